```python
import jax, jax.numpy as jnp
from jax import lax
import numpy as np

D_MODEL = 1024
BATCH = 2
SEQ = 8192
DEPTH = 4

D_A = D_MODEL
CONV_A = 31
D_B = D_MODEL
CONV_B = 3
SPLITS = (D_A, 2 * D_A, 2 * D_A + D_B, 2 * D_A + 2 * D_B, 2 * D_A + 3 * D_B,
          2 * D_A + 3 * D_B + D_MODEL)
N_IN = 2 * D_A + 3 * D_B + 2 * D_MODEL
N_EXPERTS = 32
TOP_K = 4
D_FF = D_MODEL
SWIGLU_LIMIT = 7.0
SWIGLU_ALPHA = 1.702
EXPERT_BLOCK = 256
DEEPNORM_ALPHA = (2.0 * DEPTH) ** 0.25
DEEPNORM_BETA = (8.0 * DEPTH) ** -0.25
LN_EPS = 1e-5

kernel_name = "hybrid_conformer_shortconv_moe_deepnorm_adaln"


def layer_norm(x, g=None, b=None):
    xf = x.astype(jnp.float32)
    mu = jnp.mean(xf, axis=-1, keepdims=True)
    xc = xf - mu
    var = jnp.mean(jnp.square(xc), axis=-1, keepdims=True)
    y = xc * lax.rsqrt(var + LN_EPS)
    if g is not None:
        y = y * g.astype(jnp.float32) + b.astype(jnp.float32)
    return y.astype(x.dtype)


def causal_dwconv(x, w):
    k, ch = w.shape
    return lax.conv_general_dilated(
        x, w[:, None, :].astype(x.dtype), window_strides=(1,), padding=[(k - 1, 0)],
        dimension_numbers=("NWC", "WIO", "NWC"), feature_group_count=ch)


def token_mixer(u, w_in, b_in, conv_a_w, conv_a_b, ln_a_g, ln_a_b, conv_b_w, w_pa, b_pa, w_pb, w_o):
    z = u @ w_in + b_in
    a_val, a_gate, gb, gc, hb, z_ga, z_gb = jnp.split(z, SPLITS, axis=-1)
    ya = a_val * jax.nn.sigmoid(a_gate)
    ya = causal_dwconv(ya, conv_a_w) + conv_a_b
    ya = jax.nn.silu(layer_norm(ya, ln_a_g, ln_a_b))
    ya = ya @ w_pa + b_pa
    yb = (gb * causal_dwconv(gc * hb, conv_b_w)) @ w_pb
    m = jax.nn.sigmoid(z_ga) * ya + jax.nn.sigmoid(z_gb) * yb
    return m @ w_o


def clamped_swiglu(h):
    h_glu, h_lin = jnp.split(h, 2, axis=-1)
    h_glu = jnp.minimum(h_glu, SWIGLU_LIMIT)
    h_lin = jnp.clip(h_lin, -SWIGLU_LIMIT, SWIGLU_LIMIT)
    return h_glu * jax.nn.sigmoid(SWIGLU_ALPHA * h_glu) * (h_lin + 1.0)


def moe(u, router_w, router_b, w1, b1, w2, b2):
    bsz, seq, d = u.shape
    n_tok = bsz * seq
    xf = u.reshape(n_tok, d)
    logits = (xf @ router_w + router_b).astype(jnp.float32)
    top_v, top_i = lax.top_k(logits, TOP_K)
    gate = jax.nn.softmax(top_v, axis=-1).astype(u.dtype)
    n_asg = n_tok * TOP_K
    asg_e = top_i.reshape(n_asg).astype(jnp.int32)
    asg_tok = jnp.arange(n_asg, dtype=jnp.int32) // TOP_K
    order = jnp.argsort(asg_e, stable=True)
    sorted_e = asg_e[order]
    counts = jnp.bincount(asg_e, length=N_EXPERTS).astype(jnp.int32)
    padded = (counts + EXPERT_BLOCK - 1) // EXPERT_BLOCK * EXPERT_BLOCK
    pend = jnp.cumsum(padded)
    pstart = pend - padded
    ustart = jnp.cumsum(counts) - counts
    dest = pstart[sorted_e] + jnp.arange(n_asg, dtype=jnp.int32) - ustart[sorted_e]
    n_blocks = (n_asg + N_EXPERTS * (EXPERT_BLOCK - 1) + EXPERT_BLOCK - 1) // EXPERT_BLOCK
    n_rows = n_blocks * EXPERT_BLOCK
    row_tok = jnp.zeros((n_rows,), jnp.int32).at[dest].set(asg_tok[order])
    row_w = jnp.zeros((n_rows,), u.dtype).at[dest].set(gate.reshape(n_asg)[order])
    block_e = jnp.minimum(
        jnp.searchsorted(pend, jnp.arange(n_blocks, dtype=jnp.int32) * EXPERT_BLOCK, side="right"),
        N_EXPERTS - 1)
    xs = xf[row_tok].reshape(n_blocks, EXPERT_BLOCK, d)

    def expert_block(args):
        xb, e = args
        h = clamped_swiglu(xb @ w1[e] + b1[e])
        return h @ w2[e] + b2[e]

    ys = lax.map(expert_block, (xs, block_e)).reshape(n_rows, d)
    y = jax.ops.segment_sum(ys * row_w[:, None], row_tok, num_segments=n_tok)
    return y.reshape(bsz, seq, d)


def setup_inputs(seed: int = 0) -> dict:
    key = jax.random.key(seed)
    ks = jax.random.split(key, 32)
    L, D, E, F = DEPTH, D_MODEL, N_EXPERTS, D_FF
    nrm = lambda k, shape, s: jax.random.normal(k, shape, jnp.float32) * s
    gain = lambda k, shape: 1.0 + 0.01 * jax.random.normal(k, shape, jnp.float32)
    return {
        "x": nrm(ks[0], (BATCH, SEQ, D), 1.0),
        "c": nrm(ks[1], (BATCH, D), 1.0),
        "ada_w": nrm(ks[2], (L, D, 6 * D), D ** -0.5),
        "ada_b": nrm(ks[3], (L, 6 * D), 0.01),
        "w_in": nrm(ks[4], (L, D, N_IN), D ** -0.5),
        "b_in": nrm(ks[5], (L, N_IN), 0.01),
        "conv_a_w": nrm(ks[6], (L, CONV_A, D_A), CONV_A ** -0.5),
        "conv_a_b": nrm(ks[7], (L, D_A), 0.01),
        "ln_a_g": gain(ks[8], (L, D_A)),
        "ln_a_b": nrm(ks[9], (L, D_A), 0.01),
        "conv_b_w": nrm(ks[10], (L, CONV_B, D_B), CONV_B ** -0.5),
        "w_pa": nrm(ks[11], (L, D_A, D), D_A ** -0.5 * DEEPNORM_BETA),
        "b_pa": nrm(ks[12], (L, D), 0.01),
        "w_pb": nrm(ks[13], (L, D_B, D), D_B ** -0.5 * DEEPNORM_BETA),
        "w_o": nrm(ks[14], (L, D, D), D ** -0.5 * DEEPNORM_BETA),
        "ln1_g": gain(ks[15], (L, D)),
        "ln1_b": nrm(ks[16], (L, D), 0.01),
        "router_w": nrm(ks[17], (L, D, E), D ** -0.5),
        "router_b": nrm(ks[18], (L, E), 0.01),
        "w1": nrm(ks[19], (L, E, D, 2 * F), D ** -0.5),
        "b1": nrm(ks[20], (L, E, 2 * F), 0.01),
        "w2": nrm(ks[21], (L, E, F, D), F ** -0.5 * DEEPNORM_BETA),
        "b2": nrm(ks[22], (L, E, D), 0.01),
        "ln2_g": gain(ks[23], (L, D)),
        "ln2_b": nrm(ks[24], (L, D), 0.01),
    }


def reference(x, c, ada_w, ada_b, w_in, b_in, conv_a_w, conv_a_b, ln_a_g, ln_a_b, conv_b_w,
              w_pa, b_pa, w_pb, w_o, ln1_g, ln1_b, router_w, router_b, w1, b1, w2, b2,
              ln2_g, ln2_b):
    cond = jax.nn.silu(c)
    for l in range(DEPTH):
        mod = cond @ ada_w[l] + ada_b[l]
        sh1, sc1, g1, sh2, sc2, g2 = [m[:, None, :] for m in jnp.split(mod, 6, axis=-1)]
        u = layer_norm(x) * (1.0 + sc1) + sh1
        y = token_mixer(u, w_in[l], b_in[l], conv_a_w[l], conv_a_b[l], ln_a_g[l], ln_a_b[l],
                        conv_b_w[l], w_pa[l], b_pa[l], w_pb[l], w_o[l])
        x = layer_norm(DEEPNORM_ALPHA * x + g1 * y, ln1_g[l], ln1_b[l])
        u = layer_norm(x) * (1.0 + sc2) + sh2
        y = moe(u, router_w[l], router_b[l], w1[l], b1[l], w2[l], b2[l])
        x = layer_norm(DEEPNORM_ALPHA * x + g2 * y, ln2_g[l], ln2_b[l])
    return x
```

```python
import functools

import jax
import jax.numpy as jnp
from jax import lax
from jax.experimental import pallas as pl
from jax.experimental.pallas import tpu as pltpu

D_MODEL = 1024
DEPTH = 4
CONV_A = 31
CONV_B = 3
N_EXPERTS = 32
TOP_K = 4
D_FF = D_MODEL
SWIGLU_LIMIT = 7.0
SWIGLU_ALPHA = 1.702
EXPERT_BLOCK = 256
DEEPNORM_ALPHA = (2.0 * DEPTH) ** 0.25
LN_EPS = 1e-5

LANES = 128
SUBLANES = 8
VMEM_LIMIT = 56 * 1024 * 1024

MIX_TS = 256
CONV_RC = 32
CARRY_A = 32
CARRY_B = 8
ROUTE_TM = 512
TOK_TILE = 256

F32 = jnp.float32
BF16 = jnp.bfloat16


def _ln(x):
    mu = jnp.mean(x, axis=-1, keepdims=True)
    xc = x - mu
    var = jnp.mean(xc * xc, axis=-1, keepdims=True)
    return xc * lax.rsqrt(var + LN_EPS)


def _sigmoid(x):
    return 1.0 / (1.0 + jnp.exp(-x))


def _bdot(a, b):
    return jnp.dot(a.astype(BF16), b, preferred_element_type=F32)


def _ada_kernel(c_ref, w_ref, b_ref, o_ref):
    c = c_ref[...]
    cond = c * _sigmoid(c)
    o_ref[0] = jnp.dot(cond, w_ref[0], preferred_element_type=F32,
                       precision=lax.Precision.HIGHEST) + b_ref[0]


def _ada_mod(c, ada_w, ada_b):
    bsz, d = c.shape
    n_layers = ada_w.shape[0]
    c_pad = jnp.zeros((SUBLANES, d), F32).at[:bsz].set(c)
    out = pl.pallas_call(
        _ada_kernel,
        grid=(n_layers, 6),
        in_specs=[
            pl.BlockSpec((SUBLANES, d), lambda l, j: (0, 0)),
            pl.BlockSpec((1, d, d), lambda l, j: (l, 0, j)),
            pl.BlockSpec((1, 1, d), lambda l, j: (l, 0, j)),
        ],
        out_specs=pl.BlockSpec((1, SUBLANES, d), lambda l, j: (l, 0, j)),
        out_shape=jax.ShapeDtypeStruct((n_layers, SUBLANES, 6 * d), F32),
        compiler_params=pltpu.CompilerParams(
            dimension_semantics=("arbitrary", "arbitrary"), vmem_limit_bytes=VMEM_LIMIT),
        name="ada_mod",
    )(c_pad, ada_w, ada_b.reshape(n_layers, 1, 6 * d))
    return out[:, :bsz].reshape(n_layers, bsz, 6, d)


def _mixer_kernel(x_ref, mod_ref, w_in_ref, b_in_ref, caw_ref, cab_ref, lag_ref, lab_ref,
                  cbw_ref, w_pa_ref, b_pa_ref, w_pb_ref, w_o_ref, g_ref, b_ref,
                  o_ref, ext_a, ext_b, cv_a, cv_b):
    ts = x_ref.shape[1]
    d = D_MODEL

    @pl.when(pl.program_id(1) == 0)
    def _():
        ext_a[pl.ds(0, CARRY_A), :] = jnp.zeros((CARRY_A, d), F32)
        ext_b[pl.ds(0, CARRY_B), :] = jnp.zeros((CARRY_B, d), F32)

    xt = x_ref[0]
    sh1 = mod_ref[0, 0, 0:1, :]
    sc1 = mod_ref[0, 0, 1:2, :]
    g1 = mod_ref[0, 0, 2:3, :]
    u = (_ln(xt) * (1.0 + sc1) + sh1).astype(BF16)

    za = jnp.dot(u, w_in_ref[:, 0:2 * d], preferred_element_type=F32) + b_in_ref[:, 0:2 * d]
    ext_a[pl.ds(CARRY_A, ts), :] = za[:, 0:d] * _sigmoid(za[:, d:2 * d])
    zb = jnp.dot(u, w_in_ref[:, 2 * d:5 * d], preferred_element_type=F32) + b_in_ref[:, 2 * d:5 * d]
    gb = zb[:, 0:d]
    ext_b[pl.ds(CARRY_B, ts), :] = zb[:, d:2 * d] * zb[:, 2 * d:3 * d]

    for r in range(ts // CONV_RC):
        r0 = r * CONV_RC
        acc = caw_ref[0:1, :] * ext_a[pl.ds(r0 + (CARRY_A - CONV_A + 1), CONV_RC), :]
        for k in range(1, CONV_A):
            acc = acc + caw_ref[k:k + 1, :] * ext_a[pl.ds(r0 + (CARRY_A - CONV_A + 1) + k, CONV_RC), :]
        cv_a[pl.ds(r0, CONV_RC), :] = acc
        accb = cbw_ref[0:1, :] * ext_b[pl.ds(r0 + (CARRY_B - CONV_B + 1), CONV_RC), :]
        for k in range(1, CONV_B):
            accb = accb + cbw_ref[k:k + 1, :] * ext_b[pl.ds(r0 + (CARRY_B - CONV_B + 1) + k, CONV_RC), :]
        cv_b[pl.ds(r0, CONV_RC), :] = accb

    ext_a[pl.ds(0, CARRY_A), :] = ext_a[pl.ds(ts, CARRY_A), :]
    ext_b[pl.ds(0, CARRY_B), :] = ext_b[pl.ds(ts, CARRY_B), :]

    ya = _ln(cv_a[...] + cab_ref[...]) * lag_ref[...] + lab_ref[...]
    ya = ya * _sigmoid(ya)
    ya = _bdot(ya, w_pa_ref[...]) + b_pa_ref[...]
    yb = _bdot(gb * cv_b[...], w_pb_ref[...])

    zg = jnp.dot(u, w_in_ref[:, 5 * d:7 * d], preferred_element_type=F32) + b_in_ref[:, 5 * d:7 * d]
    m = _sigmoid(zg[:, 0:d]) * ya + _sigmoid(zg[:, d:2 * d]) * yb
    y = _bdot(m, w_o_ref[...])
    o_ref[0] = _ln(DEEPNORM_ALPHA * xt + g1 * y) * g_ref[...] + b_ref[...]


def _const_spec(shape):
    nd = len(shape)
    return pl.BlockSpec(shape, lambda b, s: (0,) * nd)


def _token_mixer(x, mod, l, w_in, b_in, caw, cab, lag, lab, cbw, w_pa, b_pa, w_pb, w_o, g, b):
    bsz, seq, d = x.shape
    ts = MIX_TS
    n_in = w_in.shape[1]
    return pl.pallas_call(
        _mixer_kernel,
        grid=(bsz, seq // ts),
        in_specs=[
            pl.BlockSpec((1, ts, d), lambda b, s: (b, s, 0)),
            pl.BlockSpec((1, 1, 6, d), lambda b, s: (l, b, 0, 0)),
            _const_spec((d, n_in)),
            _const_spec((1, n_in)),
            _const_spec((CONV_A + 1, d)),
            _const_spec((1, d)),
            _const_spec((1, d)),
            _const_spec((1, d)),
            _const_spec((SUBLANES, d)),
            _const_spec((d, d)),
            _const_spec((1, d)),
            _const_spec((d, d)),
            _const_spec((d, d)),
            _const_spec((1, d)),
            _const_spec((1, d)),
        ],
        out_specs=pl.BlockSpec((1, ts, d), lambda b, s: (b, s, 0)),
        out_shape=jax.ShapeDtypeStruct((bsz, seq, d), F32),
        scratch_shapes=[
            pltpu.VMEM((ts + CARRY_A, d), F32),
            pltpu.VMEM((ts + CARRY_B, d), F32),
            pltpu.VMEM((ts, d), F32),
            pltpu.VMEM((ts, d), F32),
        ],
        compiler_params=pltpu.CompilerParams(
            dimension_semantics=("arbitrary", "arbitrary"), vmem_limit_bytes=VMEM_LIMIT),
        name="token_mixer",
    )(x, mod, w_in, b_in, caw, cab, lag, lab, cbw, w_pa, b_pa, w_pb, w_o, g, b)


def _router_kernel(x_ref, mod_ref, rw_ref, rb_ref, u2_ref, topi_ref, gate_ref, rank_ref, cnt_ref,
                   carry_ref):
    tm = x_ref.shape[0]

    @pl.when(pl.program_id(0) == 0)
    def _():
        carry_ref[...] = jnp.zeros_like(carry_ref)

    sh2 = mod_ref[0, 0, 3:4, :]
    sc2 = mod_ref[0, 0, 4:5, :]
    u2 = _ln(x_ref[...]) * (1.0 + sc2) + sh2
    u2_ref[...] = u2
    logits = jnp.dot(u2, rw_ref[...], preferred_element_type=F32,
                     precision=lax.Precision.HIGHEST) + rb_ref[...]
    lane = lax.broadcasted_iota(jnp.int32, (tm, LANES), 1)
    lane_f = lane.astype(F32)
    neg_inf = jnp.float32(-jnp.inf)
    work = jnp.where(lane < N_EXPERTS, logits, neg_inf)
    sel = jnp.zeros((tm, LANES), F32)
    vals, idxs = [], []
    for _ in range(TOP_K):
        m = jnp.max(work, axis=-1, keepdims=True)
        idx = jnp.min(jnp.where(work == m, lane_f, float(LANES)), axis=-1, keepdims=True)
        hit = lane_f == idx
        vals.append(m)
        idxs.append(idx)
        work = jnp.where(hit, neg_inf, work)
        sel = sel + hit.astype(F32)
    exps = [jnp.exp(v - vals[0]) for v in vals]
    inv = 1.0 / (exps[0] + exps[1] + exps[2] + exps[3])

    row = lax.broadcasted_iota(jnp.int32, (tm, tm), 0)
    col = lax.broadcasted_iota(jnp.int32, (tm, tm), 1)
    tri = (col < row).astype(BF16)
    prefix = jnp.dot(tri, sel.astype(BF16), preferred_element_type=F32)
    rank_dense = prefix + carry_ref[...]
    carry_ref[...] = carry_ref[...] + jnp.sum(sel, axis=0, keepdims=True)
    cnt_ref[...] = carry_ref[...]

    topi = jnp.zeros((tm, LANES), F32)
    gate = jnp.zeros((tm, LANES), F32)
    rank = jnp.zeros((tm, LANES), F32)
    for k in range(TOP_K):
        rk = jnp.sum(jnp.where(lane_f == idxs[k], rank_dense, 0.0), axis=-1, keepdims=True)
        here = lane == k
        topi = jnp.where(here, idxs[k], topi)
        gate = jnp.where(here, exps[k] * inv, gate)
        rank = jnp.where(here, rk, rank)
    topi_ref[...] = topi.astype(jnp.int32)
    gate_ref[...] = gate
    rank_ref[...] = rank.astype(jnp.int32)


def _router(x1, mod, l, rw_pad, rb_pad, seq):
    n_tok, d = x1.shape
    tm = ROUTE_TM
    tiles_per_seq = seq // tm
    tok_spec = pl.BlockSpec((tm, d), lambda i: (i, 0))
    lane_spec = pl.BlockSpec((tm, LANES), lambda i: (i, 0))
    return pl.pallas_call(
        _router_kernel,
        grid=(n_tok // tm,),
        in_specs=[
            tok_spec,
            pl.BlockSpec((1, 1, 6, d), lambda i: (l, i // tiles_per_seq, 0, 0)),
            pl.BlockSpec((d, LANES), lambda i: (0, 0)),
            pl.BlockSpec((1, LANES), lambda i: (0, 0)),
        ],
        out_specs=[tok_spec, lane_spec, lane_spec, lane_spec,
                   pl.BlockSpec((1, LANES), lambda i: (0, 0))],
        out_shape=[
            jax.ShapeDtypeStruct((n_tok, d), F32),
            jax.ShapeDtypeStruct((n_tok, LANES), jnp.int32),
            jax.ShapeDtypeStruct((n_tok, LANES), F32),
            jax.ShapeDtypeStruct((n_tok, LANES), jnp.int32),
            jax.ShapeDtypeStruct((1, LANES), F32),
        ],
        scratch_shapes=[pltpu.VMEM((1, LANES), F32)],
        compiler_params=pltpu.CompilerParams(
            dimension_semantics=("arbitrary",), vmem_limit_bytes=VMEM_LIMIT),
        name="router",
    )(x1, mod, rw_pad, rb_pad)


def _dispatch_kernel(dest_ref, u2_ref, xs_in_ref, xs_ref, sem):
    del xs_in_ref
    tt = u2_ref.shape[0]

    def issue(t, carry):
        for k in range(TOP_K):
            dst = dest_ref[0, 0, k * tt + t]
            pltpu.make_async_copy(u2_ref.at[pl.ds(t, 1)], xs_ref.at[pl.ds(dst, 1)], sem).start()
        return carry

    lax.fori_loop(0, tt, issue, 0, unroll=4)
    for _ in range(TOP_K):
        pltpu.make_async_copy(u2_ref, xs_ref.at[pl.ds(0, tt)], sem).wait()


def _dispatch(dest_tiles, u2, n_rows):
    n_tok, d = u2.shape
    tt = TOK_TILE
    xs0 = jnp.zeros((n_rows, d), F32)
    return pl.pallas_call(
        _dispatch_kernel,
        grid=(n_tok // tt,),
        in_specs=[
            pl.BlockSpec((1, 1, TOP_K * tt), lambda i: (i, 0, 0), memory_space=pltpu.SMEM),
            pl.BlockSpec((tt, d), lambda i: (i, 0)),
            pl.BlockSpec(memory_space=pl.ANY),
        ],
        out_specs=pl.BlockSpec(memory_space=pl.ANY),
        out_shape=jax.ShapeDtypeStruct((n_rows, d), F32),
        scratch_shapes=[pltpu.SemaphoreType.DMA(())],
        input_output_aliases={2: 0},
        compiler_params=pltpu.CompilerParams(
            dimension_semantics=("arbitrary",), vmem_limit_bytes=VMEM_LIMIT),
        name="dispatch",
    )(dest_tiles, u2, xs0)


def _expert_kernel(be_ref, nb_ref, xs_ref, w1_ref, b1_ref, w2_ref, b2_ref, ys_ref, w1b, w2b):
    b = pl.program_id(0)
    e = be_ref[b]
    prev = be_ref[jnp.maximum(b - 1, 0)]

    @pl.when(jnp.logical_or(b == 0, e != prev))
    def _():
        w1b[...] = w1_ref[0, 0].astype(BF16)
        w2b[...] = w2_ref[0, 0].astype(BF16)

    @pl.when(b < nb_ref[0])
    def _():
        h = _bdot(xs_ref[...], w1b[...]) + b1_ref[0, 0]
        h_glu = jnp.minimum(h[:, 0:D_FF], SWIGLU_LIMIT)
        h_lin = jnp.clip(h[:, D_FF:2 * D_FF], -SWIGLU_LIMIT, SWIGLU_LIMIT)
        a = h_glu * _sigmoid(SWIGLU_ALPHA * h_glu) * (h_lin + 1.0)
        ys_ref[...] = _bdot(a, w2b[...]) + b2_ref[0, 0]

    @pl.when(b >= nb_ref[0])
    def _():
        ys_ref[...] = jnp.zeros_like(ys_ref)


def _experts(block_e, n_used, xs, l, w1, b1, w2, b2):
    n_rows, d = xs.shape
    n_blocks = n_rows // EXPERT_BLOCK
    f2 = w1.shape[-1]
    grid_spec = pltpu.PrefetchScalarGridSpec(
        num_scalar_prefetch=2,
        grid=(n_blocks,),
        in_specs=[
            pl.BlockSpec((EXPERT_BLOCK, d), lambda i, be, nb: (i, 0)),
            pl.BlockSpec((1, 1, d, f2), lambda i, be, nb: (l, be[i], 0, 0)),
            pl.BlockSpec((1, 1, 1, f2), lambda i, be, nb: (l, be[i], 0, 0)),
            pl.BlockSpec((1, 1, D_FF, d), lambda i, be, nb: (l, be[i], 0, 0)),
            pl.BlockSpec((1, 1, 1, d), lambda i, be, nb: (l, be[i], 0, 0)),
        ],
        out_specs=pl.BlockSpec((EXPERT_BLOCK, d), lambda i, be, nb: (i, 0)),
        scratch_shapes=[pltpu.VMEM((d, f2), BF16), pltpu.VMEM((D_FF, d), BF16)],
    )
    return pl.pallas_call(
        _expert_kernel,
        grid_spec=grid_spec,
        out_shape=jax.ShapeDtypeStruct((n_rows, d), F32),
        compiler_params=pltpu.CompilerParams(
            dimension_semantics=("arbitrary",), vmem_limit_bytes=VMEM_LIMIT),
        name="experts",
    )(block_e, n_used, xs, w1, b1, w2, b2)


def _combine_kernel(dest_ref, gate_ref, x_ref, mod_ref, g_ref, b_ref, ys_ref, o_ref, buf, sem):
    tt = x_ref.shape[0]

    def issue(t, carry):
        for k in range(TOP_K):
            src = dest_ref[0, 0, k * tt + t]
            pltpu.make_async_copy(ys_ref.at[pl.ds(src, 1)], buf.at[pl.ds(k * tt + t, 1)], sem).start()
        return carry

    lax.fori_loop(0, tt, issue, 0, unroll=4)
    for k in range(TOP_K):
        pltpu.make_async_copy(ys_ref.at[pl.ds(0, tt)], buf.at[pl.ds(k * tt, tt)], sem).wait()

    y = gate_ref[:, 0:1] * buf[pl.ds(0, tt), :]
    for k in range(1, TOP_K):
        y = y + gate_ref[:, k:k + 1] * buf[pl.ds(k * tt, tt), :]
    g2 = mod_ref[0, 0, 5:6, :]
    o_ref[...] = _ln(DEEPNORM_ALPHA * x_ref[...] + g2 * y) * g_ref[...] + b_ref[...]


def _combine(dest_tiles, gate, x1, mod, l, ys, g, b, seq):
    n_tok, d = x1.shape
    tt = TOK_TILE
    tiles_per_seq = seq // tt
    return pl.pallas_call(
        _combine_kernel,
        grid=(n_tok // tt,),
        in_specs=[
            pl.BlockSpec((1, 1, TOP_K * tt), lambda i: (i, 0, 0), memory_space=pltpu.SMEM),
            pl.BlockSpec((tt, LANES), lambda i: (i, 0)),
            pl.BlockSpec((tt, d), lambda i: (i, 0)),
            pl.BlockSpec((1, 1, 6, d), lambda i: (l, i // tiles_per_seq, 0, 0)),
            pl.BlockSpec((1, d), lambda i: (0, 0)),
            pl.BlockSpec((1, d), lambda i: (0, 0)),
            pl.BlockSpec(memory_space=pl.ANY),
        ],
        out_specs=pl.BlockSpec((tt, d), lambda i: (i, 0)),
        out_shape=jax.ShapeDtypeStruct((n_tok, d), F32),
        scratch_shapes=[pltpu.VMEM((TOP_K * tt, d), F32), pltpu.SemaphoreType.DMA(())],
        compiler_params=pltpu.CompilerParams(
            dimension_semantics=("arbitrary",), vmem_limit_bytes=VMEM_LIMIT),
        name="combine",
    )(dest_tiles, gate, x1, mod, g, b, ys)


def kernel(x, c, ada_w, ada_b, w_in, b_in, conv_a_w, conv_a_b, ln_a_g, ln_a_b, conv_b_w, w_pa, b_pa,
           w_pb, w_o, ln1_g, ln1_b, router_w, router_b, w1, b1, w2, b2, ln2_g, ln2_b):
    bsz, seq, d = x.shape
    n_layers = ada_w.shape[0]
    n_tok = bsz * seq
    n_asg = n_tok * TOP_K
    n_blocks = (n_asg + N_EXPERTS * (EXPERT_BLOCK - 1) + EXPERT_BLOCK - 1) // EXPERT_BLOCK
    n_rows = n_blocks * EXPERT_BLOCK
    n_tiles = n_tok // TOK_TILE

    mod = _ada_mod(c, ada_w, ada_b)

    w_in_b = w_in.astype(BF16)
    w_pa_b = w_pa.astype(BF16)
    w_pb_b = w_pb.astype(BF16)
    w_o_b = w_o.astype(BF16)
    caw = jnp.pad(conv_a_w, ((0, 0), (0, 1), (0, 0)))
    cbw = jnp.pad(conv_b_w, ((0, 0), (0, SUBLANES - CONV_B), (0, 0)))
    rw_pad = jnp.pad(router_w, ((0, 0), (0, 0), (0, LANES - N_EXPERTS)))
    rb_pad = jnp.pad(router_b, ((0, 0), (0, LANES - N_EXPERTS)))
    b1r = b1.reshape(n_layers, N_EXPERTS, 1, 2 * D_FF)
    b2r = b2.reshape(n_layers, N_EXPERTS, 1, d)
    row = lambda a, l: a[l][None, :]

    for l in range(n_layers):
        x = _token_mixer(x, mod, l, w_in_b[l], row(b_in, l), caw[l], row(conv_a_b, l),
                         row(ln_a_g, l), row(ln_a_b, l), cbw[l], w_pa_b[l], row(b_pa, l),
                         w_pb_b[l], w_o_b[l], row(ln1_g, l), row(ln1_b, l))
        x1 = x.reshape(n_tok, d)
        u2, topi, gate, rank, cnt = _router(x1, mod, l, rw_pad[l], rb_pad[l][None, :], seq)

        counts = cnt[0, :N_EXPERTS].astype(jnp.int32)
        padded = (counts + EXPERT_BLOCK - 1) // EXPERT_BLOCK * EXPERT_BLOCK
        pend = jnp.cumsum(padded)
        pstart = pend - padded
        block_e = jnp.minimum(
            jnp.searchsorted(pend, jnp.arange(n_blocks, dtype=jnp.int32) * EXPERT_BLOCK, side="right"),
            N_EXPERTS - 1).astype(jnp.int32)
        n_used = (pend[-1:] // EXPERT_BLOCK).astype(jnp.int32)
        dest = pstart[topi[:, :TOP_K]] + rank[:, :TOP_K]
        dest_tiles = dest.reshape(n_tiles, TOK_TILE, TOP_K).transpose(0, 2, 1).reshape(
            n_tiles, 1, TOP_K * TOK_TILE)

        xs = _dispatch(dest_tiles, u2, n_rows)
        ys = _experts(block_e, n_used, xs, l, w1, b1r, w2, b2r)
        x = _combine(dest_tiles, gate, x1, mod, l, ys, row(ln2_g, l), row(ln2_b, l), seq)
        x = x.reshape(bsz, seq, d)
    return x
```

```python
import functools

import jax
import jax.numpy as jnp
from jax import lax
from jax.experimental import pallas as pl
from jax.experimental.pallas import tpu as pltpu

D_MODEL = 1024
DEPTH = 4
CONV_A = 31
CONV_B = 3
N_EXPERTS = 32
TOP_K = 4
D_FF = D_MODEL
SWIGLU_LIMIT = 7.0
SWIGLU_ALPHA = 1.702
EXPERT_BLOCK = 256
DEEPNORM_ALPHA = (2.0 * DEPTH) ** 0.25
LN_EPS = 1e-5

LANES = 128
SUBLANES = 8
VMEM_LIMIT = 56 * 1024 * 1024

MIX_TS = 256
CONV_RC = 64
CONV_LC = 256
CARRY_A = 32
CARRY_B = 8
ROUTE_TM = 512
TOK_TILE = 256

F32 = jnp.float32
BF16 = jnp.bfloat16


def _ln(x):
    mu = jnp.mean(x, axis=-1, keepdims=True)
    xc = x - mu
    var = jnp.mean(xc * xc, axis=-1, keepdims=True)
    return xc * lax.rsqrt(var + LN_EPS)


def _sigmoid(x):
    return 1.0 / (1.0 + jnp.exp(-x))


def _bdot(a, b):
    return jnp.dot(a.astype(BF16), b, preferred_element_type=F32)


def _ada_kernel(c_ref, w_ref, b_ref, o_ref):
    c = c_ref[...]
    cond = c * _sigmoid(c)
    o_ref[0] = jnp.dot(cond, w_ref[0], preferred_element_type=F32,
                       precision=lax.Precision.HIGHEST) + b_ref[0]


def _ada_mod(c, ada_w, ada_b):
    bsz, d = c.shape
    n_layers = ada_w.shape[0]
    c_pad = jnp.zeros((SUBLANES, d), F32).at[:bsz].set(c)
    out = pl.pallas_call(
        _ada_kernel,
        grid=(n_layers, 6),
        in_specs=[
            pl.BlockSpec((SUBLANES, d), lambda l, j: (0, 0)),
            pl.BlockSpec((1, d, d), lambda l, j: (l, 0, j)),
            pl.BlockSpec((1, 1, d), lambda l, j: (l, 0, j)),
        ],
        out_specs=pl.BlockSpec((1, SUBLANES, d), lambda l, j: (l, 0, j)),
        out_shape=jax.ShapeDtypeStruct((n_layers, SUBLANES, 6 * d), F32),
        compiler_params=pltpu.CompilerParams(
            dimension_semantics=("arbitrary", "arbitrary"), vmem_limit_bytes=VMEM_LIMIT),
        name="ada_mod",
    )(c_pad, ada_w, ada_b.reshape(n_layers, 1, 6 * d))
    return out[:, :bsz].reshape(n_layers, bsz, 6, d)


def _causal_dwconv(ext_ref, w_ref, out_ref, n_taps, carry_rows, ts):
    base = carry_rows - n_taps + 1
    d = out_ref.shape[1]
    for r0 in range(0, ts, CONV_RC):
        for c0 in range(0, d, CONV_LC):
            out = None
            for b in range(SUBLANES):
                win = CONV_RC if b == 0 else CONV_RC + SUBLANES
                z = None
                for k in range(n_taps):
                    if (base + k) % SUBLANES != b:
                        continue
                    a8 = base + k - b
                    term = w_ref[k:k + 1, c0:c0 + CONV_LC] * ext_ref[pl.ds(r0 + a8, win), c0:c0 + CONV_LC]
                    z = term if z is None else z + term
                if z is None:
                    continue
                if b != 0:
                    z = pltpu.roll(z, win - b, axis=0)[0:CONV_RC]
                out = z if out is None else out + z
            out_ref[pl.ds(r0, CONV_RC), c0:c0 + CONV_LC] = out


def _mixer_kernel(x_ref, mod_ref, w_in_ref, b_in_ref, caw_ref, cab_ref, lag_ref, lab_ref,
                  cbw_ref, w_pa_ref, b_pa_ref, w_pb_ref, w_o_ref, g_ref, b_ref,
                  o_ref, ext_a, ext_b, cv_a, cv_b):
    ts = x_ref.shape[1]
    d = D_MODEL

    @pl.when(pl.program_id(1) == 0)
    def _():
        ext_a[pl.ds(0, CARRY_A), :] = jnp.zeros((CARRY_A, d), F32)
        ext_b[pl.ds(0, CARRY_B), :] = jnp.zeros((CARRY_B, d), F32)

    xt = x_ref[0]
    sh1 = mod_ref[0, 0, 0:1, :]
    sc1 = mod_ref[0, 0, 1:2, :]
    g1 = mod_ref[0, 0, 2:3, :]
    u = (_ln(xt) * (1.0 + sc1) + sh1).astype(BF16)

    za = jnp.dot(u, w_in_ref[:, 0:2 * d], preferred_element_type=F32) + b_in_ref[:, 0:2 * d]
    ext_a[pl.ds(CARRY_A, ts), :] = za[:, 0:d] * _sigmoid(za[:, d:2 * d])
    zb = jnp.dot(u, w_in_ref[:, 2 * d:5 * d], preferred_element_type=F32) + b_in_ref[:, 2 * d:5 * d]
    gb = zb[:, 0:d]
    ext_b[pl.ds(CARRY_B, ts), :] = zb[:, d:2 * d] * zb[:, 2 * d:3 * d]

    _causal_dwconv(ext_a, caw_ref, cv_a, CONV_A, CARRY_A, ts)
    _causal_dwconv(ext_b, cbw_ref, cv_b, CONV_B, CARRY_B, ts)

    ext_a[pl.ds(0, CARRY_A), :] = ext_a[pl.ds(ts, CARRY_A), :]
    ext_b[pl.ds(0, CARRY_B), :] = ext_b[pl.ds(ts, CARRY_B), :]

    ya = _ln(cv_a[...] + cab_ref[...]) * lag_ref[...] + lab_ref[...]
    ya = ya * _sigmoid(ya)
    ya = _bdot(ya, w_pa_ref[...]) + b_pa_ref[...]
    yb = _bdot(gb * cv_b[...], w_pb_ref[...])

    zg = jnp.dot(u, w_in_ref[:, 5 * d:7 * d], preferred_element_type=F32) + b_in_ref[:, 5 * d:7 * d]
    m = _sigmoid(zg[:, 0:d]) * ya + _sigmoid(zg[:, d:2 * d]) * yb
    y = _bdot(m, w_o_ref[...])
    o_ref[0] = _ln(DEEPNORM_ALPHA * xt + g1 * y) * g_ref[...] + b_ref[...]


def _const_spec(shape):
    nd = len(shape)
    return pl.BlockSpec(shape, lambda b, s: (0,) * nd)


def _token_mixer(x, mod, l, w_in, b_in, caw, cab, lag, lab, cbw, w_pa, b_pa, w_pb, w_o, g, b):
    bsz, seq, d = x.shape
    ts = MIX_TS
    n_in = w_in.shape[1]
    return pl.pallas_call(
        _mixer_kernel,
        grid=(bsz, seq // ts),
        in_specs=[
            pl.BlockSpec((1, ts, d), lambda b, s: (b, s, 0)),
            pl.BlockSpec((1, 1, 6, d), lambda b, s: (l, b, 0, 0)),
            _const_spec((d, n_in)),
            _const_spec((1, n_in)),
            _const_spec((CONV_A + 1, d)),
            _const_spec((1, d)),
            _const_spec((1, d)),
            _const_spec((1, d)),
            _const_spec((SUBLANES, d)),
            _const_spec((d, d)),
            _const_spec((1, d)),
            _const_spec((d, d)),
            _const_spec((d, d)),
            _const_spec((1, d)),
            _const_spec((1, d)),
        ],
        out_specs=pl.BlockSpec((1, ts, d), lambda b, s: (b, s, 0)),
        out_shape=jax.ShapeDtypeStruct((bsz, seq, d), F32),
        scratch_shapes=[
            pltpu.VMEM((ts + CARRY_A, d), F32),
            pltpu.VMEM((ts + CARRY_B, d), F32),
            pltpu.VMEM((ts, d), F32),
            pltpu.VMEM((ts, d), F32),
        ],
        compiler_params=pltpu.CompilerParams(
            dimension_semantics=("arbitrary", "arbitrary"), vmem_limit_bytes=VMEM_LIMIT),
        name="token_mixer",
    )(x, mod, w_in, b_in, caw, cab, lag, lab, cbw, w_pa, b_pa, w_pb, w_o, g, b)


def _router_kernel(x_ref, mod_ref, rw_ref, rb_ref, u2_ref, topi_ref, gate_ref, rank_ref, cnt_ref,
                   carry_ref):
    tm = x_ref.shape[0]

    @pl.when(pl.program_id(0) == 0)
    def _():
        carry_ref[...] = jnp.zeros_like(carry_ref)

    sh2 = mod_ref[0, 0, 3:4, :]
    sc2 = mod_ref[0, 0, 4:5, :]
    u2 = _ln(x_ref[...]) * (1.0 + sc2) + sh2
    u2_ref[...] = u2
    logits = jnp.dot(u2, rw_ref[...], preferred_element_type=F32,
                     precision=lax.Precision.HIGHEST) + rb_ref[...]
    lane = lax.broadcasted_iota(jnp.int32, (tm, LANES), 1)
    lane_f = lane.astype(F32)
    neg_inf = jnp.float32(-jnp.inf)
    work = jnp.where(lane < N_EXPERTS, logits, neg_inf)
    sel = jnp.zeros((tm, LANES), F32)
    vals, idxs = [], []
    for _ in range(TOP_K):
        m = jnp.max(work, axis=-1, keepdims=True)
        idx = jnp.min(jnp.where(work == m, lane_f, float(LANES)), axis=-1, keepdims=True)
        hit = lane_f == idx
        vals.append(m)
        idxs.append(idx)
        work = jnp.where(hit, neg_inf, work)
        sel = sel + hit.astype(F32)
    exps = [jnp.exp(v - vals[0]) for v in vals]
    inv = 1.0 / (exps[0] + exps[1] + exps[2] + exps[3])

    row = lax.broadcasted_iota(jnp.int32, (tm, tm), 0)
    col = lax.broadcasted_iota(jnp.int32, (tm, tm), 1)
    tri = (col < row).astype(BF16)
    prefix = jnp.dot(tri, sel.astype(BF16), preferred_element_type=F32)
    rank_dense = prefix + carry_ref[...]
    carry_ref[...] = carry_ref[...] + jnp.sum(sel, axis=0, keepdims=True)
    cnt_ref[...] = carry_ref[...]

    topi = jnp.zeros((tm, LANES), F32)
    gate = jnp.zeros((tm, LANES), F32)
    rank = jnp.zeros((tm, LANES), F32)
    for k in range(TOP_K):
        rk = jnp.sum(jnp.where(lane_f == idxs[k], rank_dense, 0.0), axis=-1, keepdims=True)
        here = lane == k
        topi = jnp.where(here, idxs[k], topi)
        gate = jnp.where(here, exps[k] * inv, gate)
        rank = jnp.where(here, rk, rank)
    topi_ref[...] = topi.astype(jnp.int32)
    gate_ref[...] = gate
    rank_ref[...] = rank.astype(jnp.int32)


def _router(x1, mod, l, rw_pad, rb_pad, seq):
    n_tok, d = x1.shape
    tm = ROUTE_TM
    tiles_per_seq = seq // tm
    tok_spec = pl.BlockSpec((tm, d), lambda i: (i, 0))
    lane_spec = pl.BlockSpec((tm, LANES), lambda i: (i, 0))
    return pl.pallas_call(
        _router_kernel,
        grid=(n_tok // tm,),
        in_specs=[
            tok_spec,
            pl.BlockSpec((1, 1, 6, d), lambda i: (l, i // tiles_per_seq, 0, 0)),
            pl.BlockSpec((d, LANES), lambda i: (0, 0)),
            pl.BlockSpec((1, LANES), lambda i: (0, 0)),
        ],
        out_specs=[tok_spec, lane_spec, lane_spec, lane_spec,
                   pl.BlockSpec((1, LANES), lambda i: (0, 0))],
        out_shape=[
            jax.ShapeDtypeStruct((n_tok, d), F32),
            jax.ShapeDtypeStruct((n_tok, LANES), jnp.int32),
            jax.ShapeDtypeStruct((n_tok, LANES), F32),
            jax.ShapeDtypeStruct((n_tok, LANES), jnp.int32),
            jax.ShapeDtypeStruct((1, LANES), F32),
        ],
        scratch_shapes=[pltpu.VMEM((1, LANES), F32)],
        compiler_params=pltpu.CompilerParams(
            dimension_semantics=("arbitrary",), vmem_limit_bytes=VMEM_LIMIT),
        name="router",
    )(x1, mod, rw_pad, rb_pad)


def _dispatch_kernel(dest_ref, u2_ref, xs_in_ref, xs_ref, sem):
    del xs_in_ref
    tt = u2_ref.shape[0]

    def issue(t, carry):
        for k in range(TOP_K):
            dst = dest_ref[0, 0, k * tt + t]
            pltpu.make_async_copy(u2_ref.at[pl.ds(t, 1)], xs_ref.at[pl.ds(dst, 1)], sem).start()
        return carry

    lax.fori_loop(0, tt, issue, 0, unroll=4)
    for _ in range(TOP_K):
        pltpu.make_async_copy(u2_ref, xs_ref.at[pl.ds(0, tt)], sem).wait()


def _dispatch(dest_tiles, u2, n_rows):
    n_tok, d = u2.shape
    tt = TOK_TILE
    xs0 = jnp.zeros((n_rows, d), F32)
    return pl.pallas_call(
        _dispatch_kernel,
        grid=(n_tok // tt,),
        in_specs=[
            pl.BlockSpec((1, 1, TOP_K * tt), lambda i: (i, 0, 0), memory_space=pltpu.SMEM),
            pl.BlockSpec((tt, d), lambda i: (i, 0)),
            pl.BlockSpec(memory_space=pl.ANY),
        ],
        out_specs=pl.BlockSpec(memory_space=pl.ANY),
        out_shape=jax.ShapeDtypeStruct((n_rows, d), F32),
        scratch_shapes=[pltpu.SemaphoreType.DMA(())],
        input_output_aliases={2: 0},
        compiler_params=pltpu.CompilerParams(
            dimension_semantics=("arbitrary",), vmem_limit_bytes=VMEM_LIMIT),
        name="dispatch",
    )(dest_tiles, u2, xs0)


def _expert_kernel(be_ref, nb_ref, xs_ref, w1_ref, b1_ref, w2_ref, b2_ref, ys_ref, w1b, w2b):
    b = pl.program_id(0)
    e = be_ref[b]
    prev = be_ref[jnp.maximum(b - 1, 0)]

    @pl.when(jnp.logical_or(b == 0, e != prev))
    def _():
        w1b[...] = w1_ref[0, 0].astype(BF16)
        w2b[...] = w2_ref[0, 0].astype(BF16)

    @pl.when(b < nb_ref[0])
    def _():
        h = _bdot(xs_ref[...], w1b[...]) + b1_ref[0, 0]
        h_glu = jnp.minimum(h[:, 0:D_FF], SWIGLU_LIMIT)
        h_lin = jnp.clip(h[:, D_FF:2 * D_FF], -SWIGLU_LIMIT, SWIGLU_LIMIT)
        a = h_glu * _sigmoid(SWIGLU_ALPHA * h_glu) * (h_lin + 1.0)
        ys_ref[...] = _bdot(a, w2b[...]) + b2_ref[0, 0]

    @pl.when(b >= nb_ref[0])
    def _():
        ys_ref[...] = jnp.zeros_like(ys_ref)


def _experts(block_e, n_used, xs, l, w1, b1, w2, b2):
    n_rows, d = xs.shape
    n_blocks = n_rows // EXPERT_BLOCK
    f2 = w1.shape[-1]
    grid_spec = pltpu.PrefetchScalarGridSpec(
        num_scalar_prefetch=2,
        grid=(n_blocks,),
        in_specs=[
            pl.BlockSpec((EXPERT_BLOCK, d), lambda i, be, nb: (i, 0)),
            pl.BlockSpec((1, 1, d, f2), lambda i, be, nb: (l, be[i], 0, 0)),
            pl.BlockSpec((1, 1, 1, f2), lambda i, be, nb: (l, be[i], 0, 0)),
            pl.BlockSpec((1, 1, D_FF, d), lambda i, be, nb: (l, be[i], 0, 0)),
            pl.BlockSpec((1, 1, 1, d), lambda i, be, nb: (l, be[i], 0, 0)),
        ],
        out_specs=pl.BlockSpec((EXPERT_BLOCK, d), lambda i, be, nb: (i, 0)),
        scratch_shapes=[pltpu.VMEM((d, f2), BF16), pltpu.VMEM((D_FF, d), BF16)],
    )
    return pl.pallas_call(
        _expert_kernel,
        grid_spec=grid_spec,
        out_shape=jax.ShapeDtypeStruct((n_rows, d), F32),
        compiler_params=pltpu.CompilerParams(
            dimension_semantics=("arbitrary",), vmem_limit_bytes=VMEM_LIMIT),
        name="experts",
    )(block_e, n_used, xs, w1, b1, w2, b2)


def _combine_kernel(dest_ref, gate_ref, x_ref, mod_ref, g_ref, b_ref, ys_ref, o_ref, buf, sem):
    tt = x_ref.shape[0]

    def issue(t, carry):
        for k in range(TOP_K):
            src = dest_ref[0, 0, k * tt + t]
            pltpu.make_async_copy(ys_ref.at[pl.ds(src, 1)], buf.at[pl.ds(k * tt + t, 1)], sem).start()
        return carry

    lax.fori_loop(0, tt, issue, 0, unroll=4)
    for k in range(TOP_K):
        pltpu.make_async_copy(ys_ref.at[pl.ds(0, tt)], buf.at[pl.ds(k * tt, tt)], sem).wait()

    y = gate_ref[:, 0:1] * buf[pl.ds(0, tt), :]
    for k in range(1, TOP_K):
        y = y + gate_ref[:, k:k + 1] * buf[pl.ds(k * tt, tt), :]
    g2 = mod_ref[0, 0, 5:6, :]
    o_ref[...] = _ln(DEEPNORM_ALPHA * x_ref[...] + g2 * y) * g_ref[...] + b_ref[...]


def _combine(dest_tiles, gate, x1, mod, l, ys, g, b, seq):
    n_tok, d = x1.shape
    tt = TOK_TILE
    tiles_per_seq = seq // tt
    return pl.pallas_call(
        _combine_kernel,
        grid=(n_tok // tt,),
        in_specs=[
            pl.BlockSpec((1, 1, TOP_K * tt), lambda i: (i, 0, 0), memory_space=pltpu.SMEM),
            pl.BlockSpec((tt, LANES), lambda i: (i, 0)),
            pl.BlockSpec((tt, d), lambda i: (i, 0)),
            pl.BlockSpec((1, 1, 6, d), lambda i: (l, i // tiles_per_seq, 0, 0)),
            pl.BlockSpec((1, d), lambda i: (0, 0)),
            pl.BlockSpec((1, d), lambda i: (0, 0)),
            pl.BlockSpec(memory_space=pl.ANY),
        ],
        out_specs=pl.BlockSpec((tt, d), lambda i: (i, 0)),
        out_shape=jax.ShapeDtypeStruct((n_tok, d), F32),
        scratch_shapes=[pltpu.VMEM((TOP_K * tt, d), F32), pltpu.SemaphoreType.DMA(())],
        compiler_params=pltpu.CompilerParams(
            dimension_semantics=("arbitrary",), vmem_limit_bytes=VMEM_LIMIT),
        name="combine",
    )(dest_tiles, gate, x1, mod, g, b, ys)


def kernel(x, c, ada_w, ada_b, w_in, b_in, conv_a_w, conv_a_b, ln_a_g, ln_a_b, conv_b_w, w_pa, b_pa,
           w_pb, w_o, ln1_g, ln1_b, router_w, router_b, w1, b1, w2, b2, ln2_g, ln2_b):
    bsz, seq, d = x.shape
    n_layers = ada_w.shape[0]
    n_tok = bsz * seq
    n_asg = n_tok * TOP_K
    n_blocks = (n_asg + N_EXPERTS * (EXPERT_BLOCK - 1) + EXPERT_BLOCK - 1) // EXPERT_BLOCK
    n_rows = n_blocks * EXPERT_BLOCK
    n_tiles = n_tok // TOK_TILE

    mod = _ada_mod(c, ada_w, ada_b)

    w_in_b = w_in.astype(BF16)
    w_pa_b = w_pa.astype(BF16)
    w_pb_b = w_pb.astype(BF16)
    w_o_b = w_o.astype(BF16)
    caw = jnp.pad(conv_a_w, ((0, 0), (0, 1), (0, 0)))
    cbw = jnp.pad(conv_b_w, ((0, 0), (0, SUBLANES - CONV_B), (0, 0)))
    rw_pad = jnp.pad(router_w, ((0, 0), (0, 0), (0, LANES - N_EXPERTS)))
    rb_pad = jnp.pad(router_b, ((0, 0), (0, LANES - N_EXPERTS)))
    b1r = b1.reshape(n_layers, N_EXPERTS, 1, 2 * D_FF)
    b2r = b2.reshape(n_layers, N_EXPERTS, 1, d)
    row = lambda a, l: a[l][None, :]

    for l in range(n_layers):
        x = _token_mixer(x, mod, l, w_in_b[l], row(b_in, l), caw[l], row(conv_a_b, l),
                         row(ln_a_g, l), row(ln_a_b, l), cbw[l], w_pa_b[l], row(b_pa, l),
                         w_pb_b[l], w_o_b[l], row(ln1_g, l), row(ln1_b, l))
        x1 = x.reshape(n_tok, d)
        u2, topi, gate, rank, cnt = _router(x1, mod, l, rw_pad[l], rb_pad[l][None, :], seq)

        counts = cnt[0, :N_EXPERTS].astype(jnp.int32)
        padded = (counts + EXPERT_BLOCK - 1) // EXPERT_BLOCK * EXPERT_BLOCK
        pend = jnp.cumsum(padded)
        pstart = pend - padded
        block_start = jnp.arange(n_blocks, dtype=jnp.int32) * EXPERT_BLOCK
        block_e = jnp.minimum(
            jnp.sum((pend[None, :] <= block_start[:, None]).astype(jnp.int32), axis=1), N_EXPERTS - 1)
        n_used = (pend[-1:] // EXPERT_BLOCK).astype(jnp.int32)
        dest = pstart[topi[:, :TOP_K]] + rank[:, :TOP_K]
        dest_tiles = dest.reshape(n_tiles, TOK_TILE, TOP_K).transpose(0, 2, 1).reshape(
            n_tiles, 1, TOP_K * TOK_TILE)

        xs = _dispatch(dest_tiles, u2, n_rows)
        ys = _experts(block_e, n_used, xs, l, w1, b1r, w2, b2r)
        x = _combine(dest_tiles, gate, x1, mod, l, ys, row(ln2_g, l), row(ln2_b, l), seq)
        x = x.reshape(bsz, seq, d)
    return x
```

```python
import jax
import jax.numpy as jnp
from jax import lax
from jax.experimental import pallas as pl
from jax.experimental.pallas import tpu as pltpu

D_MODEL = 1024
DEPTH = 4
CONV_A = 31
CONV_B = 3
N_EXPERTS = 32
TOP_K = 4
D_FF = D_MODEL
SWIGLU_LIMIT = 7.0
SWIGLU_ALPHA = 1.702
EXPERT_BLOCK = 256
DEEPNORM_ALPHA = (2.0 * DEPTH) ** 0.25
LN_EPS = 1e-5

LANES = 128
SUBLANES = 8
VMEM_LIMIT = 56 * 1024 * 1024

MIX_TS = 256
CONV_RC = 64
CONV_LC = 256
CARRY_A = 32
CARRY_B = 8
TOK_TILE = 256
GRANULE = SUBLANES
BLOCK_GRANULES = EXPERT_BLOCK // GRANULE
N_TOKENS = 16384
TILE_ROWS = 1280
assert TILE_ROWS >= TOP_K * TOK_TILE + N_EXPERTS * (GRANULE - 1) and TILE_ROWS % LANES == 0
N_BLOCKS = -(-(N_TOKENS * TOP_K // GRANULE + (N_TOKENS // TOK_TILE) * N_EXPERTS
               + N_EXPERTS * (BLOCK_GRANULES - 1)) // BLOCK_GRANULES)

F32 = jnp.float32
BF16 = jnp.bfloat16


def _ln(x):
    mu = jnp.mean(x, axis=-1, keepdims=True)
    xc = x - mu
    var = jnp.mean(xc * xc, axis=-1, keepdims=True)
    return xc * lax.rsqrt(var + LN_EPS)


def _sigmoid(x):
    return 1.0 / (1.0 + jnp.exp(-x))


def _bdot(a, b):
    return jnp.dot(a.astype(BF16), b, preferred_element_type=F32)


def _ada_kernel(c_ref, w_ref, b_ref, o_ref):
    c = c_ref[...]
    cond = c * _sigmoid(c)
    o_ref[0] = jnp.dot(cond, w_ref[0], preferred_element_type=F32,
                       precision=lax.Precision.HIGHEST) + b_ref[0]


def _ada_mod(c, ada_w, ada_b):
    bsz, d = c.shape
    n_layers = ada_w.shape[0]
    c_pad = jnp.zeros((SUBLANES, d), F32).at[:bsz].set(c)
    out = pl.pallas_call(
        _ada_kernel,
        grid=(n_layers, 6),
        in_specs=[
            pl.BlockSpec((SUBLANES, d), lambda l, j: (0, 0)),
            pl.BlockSpec((1, d, d), lambda l, j: (l, 0, j)),
            pl.BlockSpec((1, 1, d), lambda l, j: (l, 0, j)),
        ],
        out_specs=pl.BlockSpec((1, SUBLANES, d), lambda l, j: (l, 0, j)),
        out_shape=jax.ShapeDtypeStruct((n_layers, SUBLANES, 6 * d), F32),
        compiler_params=pltpu.CompilerParams(
            dimension_semantics=("arbitrary", "arbitrary"), vmem_limit_bytes=VMEM_LIMIT),
        name="ada_mod",
    )(c_pad, ada_w, ada_b.reshape(n_layers, 1, 6 * d))
    return out[:, :bsz].reshape(n_layers, bsz, 6, d)


def _causal_dwconv(ext_ref, w_ref, out_ref, n_taps, carry_rows, ts):
    base = carry_rows - n_taps + 1
    d = out_ref.shape[1]
    for r0 in range(0, ts, CONV_RC):
        for c0 in range(0, d, CONV_LC):
            out = None
            for b in range(SUBLANES):
                win = CONV_RC if b == 0 else CONV_RC + SUBLANES
                z = None
                for k in range(n_taps):
                    if (base + k) % SUBLANES != b:
                        continue
                    a8 = base + k - b
                    term = w_ref[k:k + 1, c0:c0 + CONV_LC] * ext_ref[pl.ds(r0 + a8, win), c0:c0 + CONV_LC]
                    z = term if z is None else z + term
                if z is None:
                    continue
                if b != 0:
                    z = pltpu.roll(z, win - b, axis=0)[0:CONV_RC]
                out = z if out is None else out + z
            out_ref[pl.ds(r0, CONV_RC), c0:c0 + CONV_LC] = out


def _mixer_kernel(x_ref, mod_ref, w_in_ref, b_in_ref, caw_ref, cab_ref, lag_ref, lab_ref,
                  cbw_ref, w_pa_ref, b_pa_ref, w_pb_ref, w_o_ref, g_ref, b_ref,
                  o_ref, ext_a, ext_b, cv_a, cv_b):
    ts = x_ref.shape[1]
    d = D_MODEL

    @pl.when(pl.program_id(1) == 0)
    def _():
        ext_a[pl.ds(0, CARRY_A), :] = jnp.zeros((CARRY_A, d), F32)
        ext_b[pl.ds(0, CARRY_B), :] = jnp.zeros((CARRY_B, d), F32)

    xt = x_ref[0]
    sh1 = mod_ref[0, 0, 0:1, :]
    sc1 = mod_ref[0, 0, 1:2, :]
    g1 = mod_ref[0, 0, 2:3, :]
    u = (_ln(xt) * (1.0 + sc1) + sh1).astype(BF16)

    za = jnp.dot(u, w_in_ref[:, 0:2 * d], preferred_element_type=F32) + b_in_ref[:, 0:2 * d]
    ext_a[pl.ds(CARRY_A, ts), :] = za[:, 0:d] * _sigmoid(za[:, d:2 * d])
    zb = jnp.dot(u, w_in_ref[:, 2 * d:5 * d], preferred_element_type=F32) + b_in_ref[:, 2 * d:5 * d]
    gb = zb[:, 0:d]
    ext_b[pl.ds(CARRY_B, ts), :] = zb[:, d:2 * d] * zb[:, 2 * d:3 * d]

    _causal_dwconv(ext_a, caw_ref, cv_a, CONV_A, CARRY_A, ts)
    _causal_dwconv(ext_b, cbw_ref, cv_b, CONV_B, CARRY_B, ts)

    ext_a[pl.ds(0, CARRY_A), :] = ext_a[pl.ds(ts, CARRY_A), :]
    ext_b[pl.ds(0, CARRY_B), :] = ext_b[pl.ds(ts, CARRY_B), :]

    ya = _ln(cv_a[...] + cab_ref[...]) * lag_ref[...] + lab_ref[...]
    ya = ya * _sigmoid(ya)
    ya = _bdot(ya, w_pa_ref[...]) + b_pa_ref[...]
    yb = _bdot(gb * cv_b[...], w_pb_ref[...])

    zg = jnp.dot(u, w_in_ref[:, 5 * d:7 * d], preferred_element_type=F32) + b_in_ref[:, 5 * d:7 * d]
    m = _sigmoid(zg[:, 0:d]) * ya + _sigmoid(zg[:, d:2 * d]) * yb
    y = _bdot(m, w_o_ref[...])
    o_ref[0] = _ln(DEEPNORM_ALPHA * xt + g1 * y) * g_ref[...] + b_ref[...]


def _const_spec(shape):
    nd = len(shape)
    return pl.BlockSpec(shape, lambda b, s: (0,) * nd)


def _token_mixer(x, mod, l, w_in, b_in, caw, cab, lag, lab, cbw, w_pa, b_pa, w_pb, w_o, g, b):
    bsz, seq, d = x.shape
    ts = MIX_TS
    n_in = w_in.shape[1]
    return pl.pallas_call(
        _mixer_kernel,
        grid=(bsz, seq // ts),
        in_specs=[
            pl.BlockSpec((1, ts, d), lambda b, s: (b, s, 0)),
            pl.BlockSpec((1, 1, 6, d), lambda b, s: (l, b, 0, 0)),
            _const_spec((d, n_in)),
            _const_spec((1, n_in)),
            _const_spec((CONV_A + 1, d)),
            _const_spec((1, d)),
            _const_spec((1, d)),
            _const_spec((1, d)),
            _const_spec((SUBLANES, d)),
            _const_spec((d, d)),
            _const_spec((1, d)),
            _const_spec((d, d)),
            _const_spec((d, d)),
            _const_spec((1, d)),
            _const_spec((1, d)),
        ],
        out_specs=pl.BlockSpec((1, ts, d), lambda b, s: (b, s, 0)),
        out_shape=jax.ShapeDtypeStruct((bsz, seq, d), F32),
        scratch_shapes=[
            pltpu.VMEM((ts + CARRY_A, d), F32),
            pltpu.VMEM((ts + CARRY_B, d), F32),
            pltpu.VMEM((ts, d), F32),
            pltpu.VMEM((ts, d), F32),
        ],
        compiler_params=pltpu.CompilerParams(
            dimension_semantics=("arbitrary", "arbitrary"), vmem_limit_bytes=VMEM_LIMIT),
        name="token_mixer",
    )(x, mod, w_in, b_in, caw, cab, lag, lab, cbw, w_pa, b_pa, w_pb, w_o, g, b)


def _router_kernel(x_ref, mod_ref, rw_ref, rb_ref, xs_ref, meta_ref, tab_ref):
    tt = x_ref.shape[0]
    d = D_MODEL

    sh2 = mod_ref[0, 0, 3:4, :]
    sc2 = mod_ref[0, 0, 4:5, :]
    u2 = _ln(x_ref[...]) * (1.0 + sc2) + sh2
    logits = jnp.dot(u2, rw_ref[...], preferred_element_type=F32,
                     precision=lax.Precision.HIGHEST) + rb_ref[...]
    lane = lax.broadcasted_iota(jnp.int32, (tt, LANES), 1)
    lane_f = lane.astype(F32)
    neg_inf = jnp.float32(-jnp.inf)
    work = jnp.where(lane < N_EXPERTS, logits, neg_inf)
    sel = jnp.zeros((tt, LANES), F32)
    vals, idxs = [], []
    for _ in range(TOP_K):
        m = jnp.max(work, axis=-1, keepdims=True)
        idx = jnp.min(jnp.where(work == m, lane_f, float(LANES)), axis=-1, keepdims=True)
        hit = lane_f == idx
        vals.append(m)
        idxs.append(idx)
        work = jnp.where(hit, neg_inf, work)
        sel = sel + hit.astype(F32)
    exps = [jnp.exp(v - vals[0]) for v in vals]
    inv = 1.0 / (exps[0] + exps[1] + exps[2] + exps[3])

    n_e = jnp.sum(sel, axis=0, keepdims=True)
    c8 = jnp.floor((n_e + float(GRANULE - 1)) * (1.0 / GRANULE))
    er = lax.broadcasted_iota(jnp.int32, (LANES, LANES), 0)
    ec = lax.broadcasted_iota(jnp.int32, (LANES, LANES), 1)
    upper = (er < ec).astype(BF16)
    off8 = jnp.dot(jnp.broadcast_to(c8, (SUBLANES, LANES)).astype(BF16), upper,
                   preferred_element_type=F32)[0:1]
    row = lax.broadcasted_iota(jnp.int32, (tt, tt), 0)
    col = lax.broadcasted_iota(jnp.int32, (tt, tt), 1)
    tri = (col < row).astype(BF16)
    prefix = jnp.dot(tri, sel.astype(BF16), preferred_element_type=F32)
    slot_dense = off8 * float(GRANULE) + prefix

    meta = jnp.zeros((tt, LANES), F32)
    for k in range(TOP_K):
        pos_k = jnp.sum(jnp.where(lane_f == idxs[k], slot_dense, 0.0), axis=-1, keepdims=True)
        meta = jnp.where(lane == k, pos_k, meta)
        meta = jnp.where(lane == TOP_K + k, exps[k] * inv, meta)
    meta_ref[...] = meta

    sub = lax.broadcasted_iota(jnp.int32, (SUBLANES, LANES), 0)
    tab_ref[0] = jnp.where(sub == 0, c8, jnp.where(sub == 1, off8, 0.0))

    meta_t = meta.T
    slot_iota = lax.broadcasted_iota(jnp.int32, (TILE_ROWS, tt), 0).astype(F32)
    pt = jnp.zeros((TILE_ROWS, tt), F32)
    gm = jnp.zeros((TILE_ROWS, tt), F32)
    for k in range(TOP_K):
        hit = slot_iota == meta_t[k:k + 1, :]
        pt = pt + hit.astype(F32)
        gm = gm + jnp.where(hit, meta_t[TOP_K + k:TOP_K + k + 1, :], 0.0)
    xs_ref[:, 0:d] = jnp.dot(pt.astype(BF16), u2.astype(BF16), preferred_element_type=F32)
    gate_row = jnp.sum(gm, axis=-1, keepdims=True)
    xs_ref[:, d:d + LANES] = jnp.broadcast_to(gate_row, (TILE_ROWS, LANES))


def _router(x1, mod, l, rw_pad, rb_pad, seq):
    n_tok, d = x1.shape
    tt = TOK_TILE
    n_tiles = n_tok // tt
    tiles_per_seq = seq // tt
    return pl.pallas_call(
        _router_kernel,
        grid=(n_tiles,),
        in_specs=[
            pl.BlockSpec((tt, d), lambda i: (i, 0)),
            pl.BlockSpec((1, 1, 6, d), lambda i: (l, i // tiles_per_seq, 0, 0)),
            pl.BlockSpec((d, LANES), lambda i: (0, 0)),
            pl.BlockSpec((1, LANES), lambda i: (0, 0)),
        ],
        out_specs=[
            pl.BlockSpec((TILE_ROWS, d + LANES), lambda i: (i, 0)),
            pl.BlockSpec((tt, LANES), lambda i: (i, 0)),
            pl.BlockSpec((1, SUBLANES, LANES), lambda i: (i, 0, 0)),
        ],
        out_shape=[
            jax.ShapeDtypeStruct((n_tiles * TILE_ROWS, d + LANES), F32),
            jax.ShapeDtypeStruct((n_tok, LANES), F32),
            jax.ShapeDtypeStruct((n_tiles, SUBLANES, LANES), F32),
        ],
        compiler_params=pltpu.CompilerParams(
            dimension_semantics=("arbitrary",), vmem_limit_bytes=VMEM_LIMIT),
        name="router",
    )(x1, mod, rw_pad, rb_pad)


def _granule_plan(tab, n_tiles):
    c8 = tab[:, 0, :N_EXPERTS].astype(jnp.int32).T
    off8 = tab[:, 1, :N_EXPERTS].astype(jnp.int32).T
    per_e = jnp.sum(c8, axis=1)
    padded = (per_e + BLOCK_GRANULES - 1) // BLOCK_GRANULES * BLOCK_GRANULES
    e_end = jnp.cumsum(padded)
    e_start = e_end - padded
    strip_start = (e_start[:, None] + jnp.cumsum(c8, axis=1) - c8).reshape(-1)
    cnt_flat = c8.reshape(-1)
    off_flat = off8.reshape(-1)
    slot = jnp.arange(N_BLOCKS * BLOCK_GRANULES, dtype=jnp.int32)
    sid = jnp.sum((strip_start[None, :] <= slot[:, None]).astype(jnp.int32), axis=1) - 1
    g = slot - strip_start[sid]
    valid = g < cnt_flat[sid]
    tile = sid % n_tiles
    rows = tile * TILE_ROWS + (off_flat[sid] + g) * GRANULE
    src = jnp.where(valid, rows, 0)
    trash = n_tiles * TILE_ROWS + (slot % BLOCK_GRANULES) * GRANULE
    dst = jnp.where(valid, rows, trash)
    block_start = jnp.arange(N_BLOCKS, dtype=jnp.int32) * BLOCK_GRANULES
    block_e = jnp.minimum(
        jnp.sum((e_end[None, :] <= block_start[:, None]).astype(jnp.int32), axis=1), N_EXPERTS - 1)
    n_used = e_end[-1:] // BLOCK_GRANULES
    return block_e.astype(jnp.int32), n_used.astype(jnp.int32), src.astype(jnp.int32), dst.astype(jnp.int32)


def _granule_copy(hbm_ref, row, buf_ref, slot, i, sem, to_hbm):
    hbm = hbm_ref.at[pl.ds(pl.multiple_of(row, GRANULE), GRANULE)]
    vmem = buf_ref.at[slot, pl.ds(i * GRANULE, GRANULE)]
    return pltpu.make_async_copy(vmem, hbm, sem) if to_hbm else pltpu.make_async_copy(hbm, vmem, sem)


def _block_copy(hbm_ref, buf_ref, slot, sem, to_hbm):
    hbm = hbm_ref.at[pl.ds(0, EXPERT_BLOCK)]
    vmem = buf_ref.at[slot]
    return pltpu.make_async_copy(vmem, hbm, sem) if to_hbm else pltpu.make_async_copy(hbm, vmem, sem)


def _expert_kernel(be_ref, nb_ref, src_ref, dst_ref, xs_ref, w1_ref, b1_ref, w2_ref, b2_ref, ys_in_ref,
                   ys_ref, w1b, w2b, xbuf, ybuf, in_sem, out_sem):
    del ys_in_ref
    b = pl.program_id(0)
    n_used = nb_ref[0]
    e = be_ref[b]
    prev = be_ref[jnp.maximum(b - 1, 0)]
    slot = b % 2

    def start_gather(blk, s):
        for i in range(BLOCK_GRANULES):
            _granule_copy(xs_ref, src_ref[blk * BLOCK_GRANULES + i], xbuf, s, i, in_sem.at[s], False).start()

    @pl.when(jnp.logical_or(b == 0, e != prev))
    def _():
        w1b[...] = w1_ref[0, 0].astype(BF16)
        w2b[...] = w2_ref[0, 0].astype(BF16)

    @pl.when(b == 0)
    def _():
        start_gather(0, 0)

    @pl.when(b + 1 < n_used)
    def _():
        start_gather(b + 1, 1 - slot)

    @pl.when(b < n_used)
    def _():
        _block_copy(xs_ref, xbuf, slot, in_sem.at[slot], False).wait()
        xb = xbuf[slot]
        h = _bdot(xb[:, 0:D_MODEL], w1b[...]) + b1_ref[0, 0]
        h_glu = jnp.minimum(h[:, 0:D_FF], SWIGLU_LIMIT)
        h_lin = jnp.clip(h[:, D_FF:2 * D_FF], -SWIGLU_LIMIT, SWIGLU_LIMIT)
        a = h_glu * _sigmoid(SWIGLU_ALPHA * h_glu) * (h_lin + 1.0)
        y = (_bdot(a, w2b[...]) + b2_ref[0, 0]) * xb[:, D_MODEL:D_MODEL + 1]

        @pl.when(b >= 2)
        def _():
            _block_copy(ys_ref, ybuf, slot, out_sem.at[slot], True).wait()

        ybuf[slot] = y
        for i in range(BLOCK_GRANULES):
            _granule_copy(ys_ref, dst_ref[b * BLOCK_GRANULES + i], ybuf, slot, i, out_sem.at[slot], True).start()

    @pl.when(b == n_used - 1)
    def _():
        @pl.when(b >= 1)
        def _():
            _block_copy(ys_ref, ybuf, 1 - slot, out_sem.at[1 - slot], True).wait()

        _block_copy(ys_ref, ybuf, slot, out_sem.at[slot], True).wait()


def _experts(block_e, n_used, src, dst, xs, l, w1, b1, w2, b2, n_tiles):
    d = D_MODEL
    f2 = w1.shape[-1]
    ys_rows = n_tiles * TILE_ROWS + EXPERT_BLOCK
    ys0 = jnp.zeros((ys_rows, d), F32)
    grid_spec = pltpu.PrefetchScalarGridSpec(
        num_scalar_prefetch=4,
        grid=(N_BLOCKS,),
        in_specs=[
            pl.BlockSpec(memory_space=pl.ANY),
            pl.BlockSpec((1, 1, d, f2), lambda i, be, nb, s, t: (l, be[i], 0, 0)),
            pl.BlockSpec((1, 1, 1, f2), lambda i, be, nb, s, t: (l, be[i], 0, 0)),
            pl.BlockSpec((1, 1, D_FF, d), lambda i, be, nb, s, t: (l, be[i], 0, 0)),
            pl.BlockSpec((1, 1, 1, d), lambda i, be, nb, s, t: (l, be[i], 0, 0)),
            pl.BlockSpec(memory_space=pl.ANY),
        ],
        out_specs=pl.BlockSpec(memory_space=pl.ANY),
        scratch_shapes=[
            pltpu.VMEM((d, f2), BF16),
            pltpu.VMEM((D_FF, d), BF16),
            pltpu.VMEM((2, EXPERT_BLOCK, d + LANES), F32),
            pltpu.VMEM((2, EXPERT_BLOCK, d), F32),
            pltpu.SemaphoreType.DMA((2,)),
            pltpu.SemaphoreType.DMA((2,)),
        ],
    )
    return pl.pallas_call(
        _expert_kernel,
        grid_spec=grid_spec,
        out_shape=jax.ShapeDtypeStruct((ys_rows, d), F32),
        input_output_aliases={9: 0},
        compiler_params=pltpu.CompilerParams(
            dimension_semantics=("arbitrary",), vmem_limit_bytes=VMEM_LIMIT),
        name="experts",
    )(block_e, n_used, src, dst, xs, w1, b1, w2, b2, ys0)


def _combine_kernel(ys_ref, meta_ref, x_ref, mod_ref, g_ref, b_ref, o_ref):
    tt = x_ref.shape[0]
    slot_iota = lax.broadcasted_iota(jnp.int32, (tt, TILE_ROWS), 1).astype(F32)
    p = jnp.zeros((tt, TILE_ROWS), F32)
    for k in range(TOP_K):
        p = p + (slot_iota == meta_ref[:, k:k + 1]).astype(F32)
    y = jnp.dot(p.astype(BF16), ys_ref[...].astype(BF16), preferred_element_type=F32)
    g2 = mod_ref[0, 0, 5:6, :]
    o_ref[...] = _ln(DEEPNORM_ALPHA * x_ref[...] + g2 * y) * g_ref[...] + b_ref[...]


def _combine(ys, meta, x1, mod, l, g, b, seq):
    n_tok, d = x1.shape
    tt = TOK_TILE
    tiles_per_seq = seq // tt
    return pl.pallas_call(
        _combine_kernel,
        grid=(n_tok // tt,),
        in_specs=[
            pl.BlockSpec((TILE_ROWS, d), lambda i: (i, 0)),
            pl.BlockSpec((tt, LANES), lambda i: (i, 0)),
            pl.BlockSpec((tt, d), lambda i: (i, 0)),
            pl.BlockSpec((1, 1, 6, d), lambda i: (l, i // tiles_per_seq, 0, 0)),
            pl.BlockSpec((1, d), lambda i: (0, 0)),
            pl.BlockSpec((1, d), lambda i: (0, 0)),
        ],
        out_specs=pl.BlockSpec((tt, d), lambda i: (i, 0)),
        out_shape=jax.ShapeDtypeStruct((n_tok, d), F32),
        compiler_params=pltpu.CompilerParams(
            dimension_semantics=("arbitrary",), vmem_limit_bytes=VMEM_LIMIT),
        name="combine",
    )(ys, meta, x1, mod, g, b)


def kernel(x, c, ada_w, ada_b, w_in, b_in, conv_a_w, conv_a_b, ln_a_g, ln_a_b, conv_b_w, w_pa, b_pa,
           w_pb, w_o, ln1_g, ln1_b, router_w, router_b, w1, b1, w2, b2, ln2_g, ln2_b):
    bsz, seq, d = x.shape
    n_layers = ada_w.shape[0]
    n_tok = bsz * seq
    n_tiles = n_tok // TOK_TILE
    assert (n_tok, d) == (N_TOKENS, D_MODEL)

    mod = _ada_mod(c, ada_w, ada_b)

    w_in_b = w_in.astype(BF16)
    w_pa_b = w_pa.astype(BF16)
    w_pb_b = w_pb.astype(BF16)
    w_o_b = w_o.astype(BF16)
    caw = jnp.pad(conv_a_w, ((0, 0), (0, 1), (0, 0)))
    cbw = jnp.pad(conv_b_w, ((0, 0), (0, SUBLANES - CONV_B), (0, 0)))
    rw_pad = jnp.pad(router_w, ((0, 0), (0, 0), (0, LANES - N_EXPERTS)))
    rb_pad = jnp.pad(router_b, ((0, 0), (0, LANES - N_EXPERTS)))
    b1r = b1.reshape(n_layers, N_EXPERTS, 1, 2 * D_FF)
    b2r = b2.reshape(n_layers, N_EXPERTS, 1, d)
    row = lambda a, l: a[l][None, :]

    for l in range(n_layers):
        x = _token_mixer(x, mod, l, w_in_b[l], row(b_in, l), caw[l], row(conv_a_b, l),
                         row(ln_a_g, l), row(ln_a_b, l), cbw[l], w_pa_b[l], row(b_pa, l),
                         w_pb_b[l], w_o_b[l], row(ln1_g, l), row(ln1_b, l))
        x1 = x.reshape(n_tok, d)
        xs, meta, tab = _router(x1, mod, l, rw_pad[l], rb_pad[l][None, :], seq)
        block_e, n_used, src, dst = _granule_plan(tab, n_tiles)
        ys = _experts(block_e, n_used, src, dst, xs, l, w1, b1r, w2, b2r, n_tiles)
        x = _combine(ys, meta, x1, mod, l, row(ln2_g, l), row(ln2_b, l), seq)
        x = x.reshape(bsz, seq, d)
    return x
```

```python
import jax
import jax.numpy as jnp
from jax import lax
from jax.experimental import pallas as pl
from jax.experimental.pallas import tpu as pltpu

D_MODEL = 1024
DEPTH = 4
CONV_A = 31
CONV_B = 3
N_EXPERTS = 32
TOP_K = 4
D_FF = D_MODEL
SWIGLU_LIMIT = 7.0
SWIGLU_ALPHA = 1.702
EXPERT_BLOCK = 256
DEEPNORM_ALPHA = (2.0 * DEPTH) ** 0.25
LN_EPS = 1e-5

LANES = 128
SUBLANES = 8
VMEM_LIMIT = 56 * 1024 * 1024

MIX_TS = 256
CONV_RC = 64
CONV_LC = 256
CARRY_A = 32
CARRY_B = 8
TOK_TILE = 256
GRANULE = SUBLANES
BLOCK_GRANULES = EXPERT_BLOCK // GRANULE
N_TOKENS = 16384
TILE_ROWS = 1280
assert TILE_ROWS >= TOP_K * TOK_TILE + N_EXPERTS * (GRANULE - 1) and TILE_ROWS % LANES == 0
N_BLOCKS = -(-((N_TOKENS // TOK_TILE) * (TILE_ROWS // GRANULE)
               + (N_EXPERTS + 1) * (BLOCK_GRANULES - 1)) // BLOCK_GRANULES)

F32 = jnp.float32
BF16 = jnp.bfloat16


def _ln(x):
    mu = jnp.mean(x, axis=-1, keepdims=True)
    xc = x - mu
    var = jnp.mean(xc * xc, axis=-1, keepdims=True)
    return xc * lax.rsqrt(var + LN_EPS)


NEG_LOG2E = -1.4426950408889634


def _sigmoid(x, scale=1.0):
    return 1.0 / (1.0 + jnp.exp2(x * (scale * NEG_LOG2E)))


def _bdot(a, b):
    return jnp.dot(a.astype(BF16), b, preferred_element_type=F32)


def _ada_kernel(c_ref, w_ref, b_ref, o_ref):
    c = c_ref[...]
    cond = c * _sigmoid(c)
    o_ref[0] = jnp.dot(cond, w_ref[0], preferred_element_type=F32,
                       precision=lax.Precision.HIGHEST) + b_ref[0]


def _ada_mod(c, ada_w, ada_b):
    bsz, d = c.shape
    n_layers = ada_w.shape[0]
    c_pad = jnp.zeros((SUBLANES, d), F32).at[:bsz].set(c)
    out = pl.pallas_call(
        _ada_kernel,
        grid=(n_layers, 6),
        in_specs=[
            pl.BlockSpec((SUBLANES, d), lambda l, j: (0, 0)),
            pl.BlockSpec((1, d, d), lambda l, j: (l, 0, j)),
            pl.BlockSpec((1, 1, d), lambda l, j: (l, 0, j)),
        ],
        out_specs=pl.BlockSpec((1, SUBLANES, d), lambda l, j: (l, 0, j)),
        out_shape=jax.ShapeDtypeStruct((n_layers, SUBLANES, 6 * d), F32),
        compiler_params=pltpu.CompilerParams(
            dimension_semantics=("arbitrary", "arbitrary"), vmem_limit_bytes=VMEM_LIMIT),
        name="ada_mod",
    )(c_pad, ada_w, ada_b.reshape(n_layers, 1, 6 * d))
    return out[:, :bsz].reshape(n_layers, bsz, 6, d)


def _causal_dwconv(ext_ref, w_ref, out_ref, n_taps, carry_rows, ts):
    base = carry_rows - n_taps + 1
    d = out_ref.shape[1]
    for r0 in range(0, ts, CONV_RC):
        for c0 in range(0, d, CONV_LC):
            out = None
            for b in range(SUBLANES):
                win = CONV_RC if b == 0 else CONV_RC + SUBLANES
                z = None
                for k in range(n_taps):
                    if (base + k) % SUBLANES != b:
                        continue
                    a8 = base + k - b
                    w8 = w_ref[pl.ds(SUBLANES * k, SUBLANES), c0:c0 + CONV_LC]
                    term = jnp.tile(w8, (win // SUBLANES, 1)) * ext_ref[pl.ds(r0 + a8, win), c0:c0 + CONV_LC]
                    z = term if z is None else z + term
                if z is None:
                    continue
                if b != 0:
                    z = pltpu.roll(z, win - b, axis=0)[0:CONV_RC]
                out = z if out is None else out + z
            out_ref[pl.ds(r0, CONV_RC), c0:c0 + CONV_LC] = out


def _mixer_kernel(x_ref, mod_ref, w_in_ref, b_in_ref, caw_ref, cab_ref, lag_ref, lab_ref,
                  cbw_ref, w_pa_ref, b_pa_ref, w_pb_ref, w_o_ref, g_ref, b_ref,
                  o_ref, ext_a, ext_b, cv_a, cv_b):
    ts = x_ref.shape[1]
    d = D_MODEL

    @pl.when(pl.program_id(1) == 0)
    def _():
        ext_a[pl.ds(0, CARRY_A), :] = jnp.zeros((CARRY_A, d), F32)
        ext_b[pl.ds(0, CARRY_B), :] = jnp.zeros((CARRY_B, d), F32)

    xt = x_ref[0]
    sh1 = mod_ref[0, 0, 0:1, :]
    sc1 = mod_ref[0, 0, 1:2, :]
    g1 = mod_ref[0, 0, 2:3, :]
    u = (_ln(xt) * (1.0 + sc1) + sh1).astype(BF16)

    za = jnp.dot(u, w_in_ref[:, 0:2 * d], preferred_element_type=F32) + b_in_ref[:, 0:2 * d]
    ext_a[pl.ds(CARRY_A, ts), :] = za[:, 0:d] * _sigmoid(za[:, d:2 * d])
    zb = jnp.dot(u, w_in_ref[:, 2 * d:5 * d], preferred_element_type=F32) + b_in_ref[:, 2 * d:5 * d]
    gb = zb[:, 0:d]
    ext_b[pl.ds(CARRY_B, ts), :] = zb[:, d:2 * d] * zb[:, 2 * d:3 * d]

    _causal_dwconv(ext_a, caw_ref, cv_a, CONV_A, CARRY_A, ts)
    _causal_dwconv(ext_b, cbw_ref, cv_b, CONV_B, CARRY_B, ts)

    ext_a[pl.ds(0, CARRY_A), :] = ext_a[pl.ds(ts, CARRY_A), :]
    ext_b[pl.ds(0, CARRY_B), :] = ext_b[pl.ds(ts, CARRY_B), :]

    ya = _ln(cv_a[...] + cab_ref[...]) * lag_ref[...] + lab_ref[...]
    ya = ya * _sigmoid(ya)
    ya = _bdot(ya, w_pa_ref[...]) + b_pa_ref[...]
    yb = _bdot(gb * cv_b[...], w_pb_ref[...])

    zg = jnp.dot(u, w_in_ref[:, 5 * d:7 * d], preferred_element_type=F32) + b_in_ref[:, 5 * d:7 * d]
    m = _sigmoid(zg[:, 0:d]) * ya + _sigmoid(zg[:, d:2 * d]) * yb
    y = _bdot(m, w_o_ref[...])
    o_ref[0] = _ln(DEEPNORM_ALPHA * xt + g1 * y) * g_ref[...] + b_ref[...]


def _layer_spec(l, shape):
    nd = len(shape)
    return pl.BlockSpec((None,) + shape, lambda b, s: (l,) + (0,) * nd)


def _token_mixer(x, mod, l, w_in, b_in, caw, cab, lag, lab, cbw, w_pa, b_pa, w_pb, w_o, g, b):
    bsz, seq, d = x.shape
    ts = MIX_TS
    n_in = w_in.shape[-1]
    return pl.pallas_call(
        _mixer_kernel,
        grid=(bsz, seq // ts),
        in_specs=[
            pl.BlockSpec((1, ts, d), lambda b, s: (b, s, 0)),
            pl.BlockSpec((1, 1, 6, d), lambda b, s: (l, b, 0, 0)),
            _layer_spec(l, (d, n_in)),
            _layer_spec(l, (1, n_in)),
            _layer_spec(l, (SUBLANES * CONV_A, d)),
            _layer_spec(l, (1, d)),
            _layer_spec(l, (1, d)),
            _layer_spec(l, (1, d)),
            _layer_spec(l, (SUBLANES * CONV_B, d)),
            _layer_spec(l, (d, d)),
            _layer_spec(l, (1, d)),
            _layer_spec(l, (d, d)),
            _layer_spec(l, (d, d)),
            _layer_spec(l, (1, d)),
            _layer_spec(l, (1, d)),
        ],
        out_specs=pl.BlockSpec((1, ts, d), lambda b, s: (b, s, 0)),
        out_shape=jax.ShapeDtypeStruct((bsz, seq, d), F32),
        scratch_shapes=[
            pltpu.VMEM((ts + CARRY_A, d), F32),
            pltpu.VMEM((ts + CARRY_B, d), F32),
            pltpu.VMEM((ts, d), F32),
            pltpu.VMEM((ts, d), F32),
        ],
        compiler_params=pltpu.CompilerParams(
            dimension_semantics=("arbitrary", "arbitrary"), vmem_limit_bytes=VMEM_LIMIT),
        name="token_mixer",
    )(x, mod, w_in, b_in, caw, cab, lag, lab, cbw, w_pa, b_pa, w_pb, w_o, g, b)


def _router_kernel(x_ref, mod_ref, rwh_ref, rwl_ref, rb_ref, xs_ref, meta_ref, tab_ref):
    tt = x_ref.shape[0]
    d = D_MODEL

    sh2 = mod_ref[0, 0, 3:4, :]
    sc2 = mod_ref[0, 0, 4:5, :]
    u2 = _ln(x_ref[...]) * (1.0 + sc2) + sh2
    u2h = u2.astype(BF16)
    u2l = (u2 - u2h.astype(F32)).astype(BF16)
    logits = (jnp.dot(u2h, rwh_ref[...], preferred_element_type=F32)
              + jnp.dot(u2l, rwh_ref[...], preferred_element_type=F32)
              + jnp.dot(u2h, rwl_ref[...], preferred_element_type=F32)) + rb_ref[...]
    lane = lax.broadcasted_iota(jnp.int32, (tt, LANES), 1)
    lane_f = lane.astype(F32)
    neg_inf = jnp.float32(-jnp.inf)
    work = jnp.where(lane < N_EXPERTS, logits, neg_inf)
    sel = jnp.zeros((tt, LANES), F32)
    vals, idxs = [], []
    for _ in range(TOP_K):
        m = jnp.max(work, axis=-1, keepdims=True)
        idx = jnp.min(jnp.where(work == m, lane_f, float(LANES)), axis=-1, keepdims=True)
        hit = lane_f == idx
        vals.append(m)
        idxs.append(idx)
        work = jnp.where(hit, neg_inf, work)
        sel = sel + hit.astype(F32)
    exps = [jnp.exp(v - vals[0]) for v in vals]
    inv = 1.0 / (exps[0] + exps[1] + exps[2] + exps[3])

    n_e = jnp.sum(sel, axis=0, keepdims=True)
    c8 = jnp.floor((n_e + float(GRANULE - 1)) * (1.0 / GRANULE))
    er = lax.broadcasted_iota(jnp.int32, (LANES, LANES), 0)
    ec = lax.broadcasted_iota(jnp.int32, (LANES, LANES), 1)
    upper = (er < ec).astype(BF16)
    off8 = jnp.dot(jnp.broadcast_to(c8, (SUBLANES, LANES)).astype(BF16), upper,
                   preferred_element_type=F32)[0:1]
    row = lax.broadcasted_iota(jnp.int32, (tt, tt), 0)
    col = lax.broadcasted_iota(jnp.int32, (tt, tt), 1)
    tri = (col < row).astype(BF16)
    prefix = jnp.dot(tri, sel.astype(BF16), preferred_element_type=F32)
    slot_dense = off8 * float(GRANULE) + prefix

    meta = jnp.zeros((tt, LANES), F32)
    for k in range(TOP_K):
        pos_k = jnp.sum(jnp.where(lane_f == idxs[k], slot_dense, 0.0), axis=-1, keepdims=True)
        meta = jnp.where(lane == k, pos_k, meta)
        meta = jnp.where(lane == TOP_K + k, exps[k] * inv, meta)
    meta_ref[...] = meta

    sub = lax.broadcasted_iota(jnp.int32, (SUBLANES, LANES), 0)
    tab_ref[0] = jnp.where(sub == 0, c8, jnp.where(sub == 1, off8, 0.0))

    meta_t = meta.T
    slot_iota = lax.broadcasted_iota(jnp.int32, (TILE_ROWS, tt), 0).astype(F32)
    pt = jnp.zeros((TILE_ROWS, tt), F32)
    gm = jnp.zeros((TILE_ROWS, tt), F32)
    for k in range(TOP_K):
        hit = slot_iota == meta_t[k:k + 1, :]
        pt = pt + hit.astype(F32)
        gm = gm + jnp.where(hit, meta_t[TOP_K + k:TOP_K + k + 1, :], 0.0)
    xs_ref[:, 0:d] = jnp.dot(pt.astype(BF16), u2.astype(BF16), preferred_element_type=F32)
    gate_row = jnp.sum(gm, axis=-1, keepdims=True)
    xs_ref[:, d:d + LANES] = jnp.broadcast_to(gate_row, (TILE_ROWS, LANES))


def _router(x1, mod, l, rw_hi, rw_lo, rb_pad, seq):
    n_tok, d = x1.shape
    tt = TOK_TILE
    n_tiles = n_tok // tt
    tiles_per_seq = seq // tt
    return pl.pallas_call(
        _router_kernel,
        grid=(n_tiles,),
        in_specs=[
            pl.BlockSpec((tt, d), lambda i: (i, 0)),
            pl.BlockSpec((1, 1, 6, d), lambda i: (l, i // tiles_per_seq, 0, 0)),
            pl.BlockSpec((None, d, LANES), lambda i: (l, 0, 0)),
            pl.BlockSpec((None, d, LANES), lambda i: (l, 0, 0)),
            pl.BlockSpec((None, 1, LANES), lambda i: (l, 0, 0)),
        ],
        out_specs=[
            pl.BlockSpec((TILE_ROWS, d + LANES), lambda i: (i, 0)),
            pl.BlockSpec((tt, LANES), lambda i: (i, 0)),
            pl.BlockSpec((1, SUBLANES, LANES), lambda i: (i, 0, 0)),
        ],
        out_shape=[
            jax.ShapeDtypeStruct((n_tiles * TILE_ROWS, d + LANES), F32),
            jax.ShapeDtypeStruct((n_tok, LANES), F32),
            jax.ShapeDtypeStruct((n_tiles, SUBLANES, LANES), F32),
        ],
        compiler_params=pltpu.CompilerParams(
            dimension_semantics=("arbitrary",), vmem_limit_bytes=VMEM_LIMIT),
        name="router",
    )(x1, mod, rw_hi, rw_lo, rb_pad)


def _granule_plan(tab, n_tiles):
    c8 = tab[:, 0, :N_EXPERTS].astype(jnp.int32).T
    off8 = tab[:, 1, :N_EXPERTS].astype(jnp.int32).T
    used = jnp.sum(c8, axis=0, keepdims=True)
    c8 = jnp.concatenate([c8, TILE_ROWS // GRANULE - used], axis=0)
    off8 = jnp.concatenate([off8, used], axis=0)
    per_e = jnp.sum(c8, axis=1)
    padded = (per_e + BLOCK_GRANULES - 1) // BLOCK_GRANULES * BLOCK_GRANULES
    e_end = jnp.cumsum(padded)
    e_start = e_end - padded
    strip_start = (e_start[:, None] + jnp.cumsum(c8, axis=1) - c8).reshape(-1)
    cnt_flat = c8.reshape(-1)
    off_flat = off8.reshape(-1)
    slot = jnp.arange(N_BLOCKS * BLOCK_GRANULES, dtype=jnp.int32)
    before = strip_start[None, :] <= slot[:, None]
    tile_flat = jnp.arange(c8.size, dtype=jnp.int32) % n_tiles

    def at_strip(table):
        diff = table - jnp.concatenate([jnp.zeros((1,), jnp.int32), table[:-1]])
        return jnp.sum(jnp.where(before, diff[None, :], 0), axis=1)

    g = slot - at_strip(strip_start)
    valid = g < at_strip(cnt_flat)
    rows = at_strip(tile_flat * TILE_ROWS + off_flat * GRANULE) + g * GRANULE
    src = jnp.where(valid, rows, 0)
    trash = n_tiles * TILE_ROWS + (slot % BLOCK_GRANULES) * GRANULE
    dst = jnp.where(valid, rows, trash)
    block_start = jnp.arange(N_BLOCKS, dtype=jnp.int32) * BLOCK_GRANULES
    block_e = jnp.minimum(
        jnp.sum((e_end[None, :] <= block_start[:, None]).astype(jnp.int32), axis=1), N_EXPERTS - 1)
    n_used = e_end[-1:] // BLOCK_GRANULES
    n_real = e_end[N_EXPERTS - 1:N_EXPERTS] // BLOCK_GRANULES
    i32 = lambda a: a.astype(jnp.int32)
    return i32(block_e), i32(n_used), i32(n_real), i32(src), i32(dst)


def _granule_copy(hbm_ref, row, buf_ref, slot, i, sem, to_hbm):
    hbm = hbm_ref.at[pl.ds(pl.multiple_of(row, GRANULE), GRANULE)]
    vmem = buf_ref.at[slot, pl.ds(i * GRANULE, GRANULE)]
    return pltpu.make_async_copy(vmem, hbm, sem) if to_hbm else pltpu.make_async_copy(hbm, vmem, sem)


def _block_copy(hbm_ref, buf_ref, slot, sem, to_hbm):
    hbm = hbm_ref.at[pl.ds(0, EXPERT_BLOCK)]
    vmem = buf_ref.at[slot]
    return pltpu.make_async_copy(vmem, hbm, sem) if to_hbm else pltpu.make_async_copy(hbm, vmem, sem)


def _expert_kernel(be_ref, nb_ref, nr_ref, src_ref, dst_ref, xs_ref, w1_ref, b1_ref, w2_ref, b2_ref,
                   ys_ref, w1b, w2b, xbuf, ybuf, in_sem, out_sem):
    b = pl.program_id(0)
    n_used = nb_ref[0]
    n_real = nr_ref[0]
    e = be_ref[b]
    prev = be_ref[jnp.maximum(b - 1, 0)]
    slot = b % 2

    def start_gather(blk, s):
        for i in range(BLOCK_GRANULES):
            _granule_copy(xs_ref, src_ref[blk * BLOCK_GRANULES + i], xbuf, s, i, in_sem.at[s], False).start()

    @pl.when(jnp.logical_or(b == 0, e != prev))
    def _():
        w1b[...] = w1_ref[0, 0].astype(BF16)
        w2b[...] = w2_ref[0, 0].astype(BF16)

    @pl.when(b == 0)
    def _():
        start_gather(0, 0)
        ybuf[1] = jnp.zeros(ybuf.shape[1:], F32)
        pad_rows = ys_ref.at[pl.ds(ys_ref.shape[0] - EXPERT_BLOCK, EXPERT_BLOCK)]
        fill = pltpu.make_async_copy(ybuf.at[1], pad_rows, out_sem.at[1])
        fill.start()
        fill.wait()

    @pl.when(b + 1 < n_used)
    def _():
        start_gather(b + 1, 1 - slot)

    @pl.when(b < n_used)
    def _():
        _block_copy(xs_ref, xbuf, slot, in_sem.at[slot], False).wait()

        @pl.when(b >= 2)
        def _():
            _block_copy(ys_ref, ybuf, slot, out_sem.at[slot], True).wait()

        @pl.when(b < n_real)
        def _():
            xb = xbuf[slot]
            h = _bdot(xb[:, 0:D_MODEL], w1b[...]) + b1_ref[0, 0]
            h_glu = jnp.minimum(h[:, 0:D_FF], SWIGLU_LIMIT)
            h_lin = jnp.clip(h[:, D_FF:2 * D_FF], -SWIGLU_LIMIT, SWIGLU_LIMIT)
            a = h_glu * _sigmoid(h_glu, SWIGLU_ALPHA) * (h_lin + 1.0)
            ybuf[slot] = (_bdot(a, w2b[...]) + b2_ref[0, 0]) * xb[:, D_MODEL:D_MODEL + 1]

        @pl.when(b >= n_real)
        def _():
            ybuf[slot] = jnp.zeros(ybuf.shape[1:], F32)

        for i in range(BLOCK_GRANULES):
            _granule_copy(ys_ref, dst_ref[b * BLOCK_GRANULES + i], ybuf, slot, i, out_sem.at[slot], True).start()

    @pl.when(b == n_used - 1)
    def _():
        @pl.when(b >= 1)
        def _():
            _block_copy(ys_ref, ybuf, 1 - slot, out_sem.at[1 - slot], True).wait()

        _block_copy(ys_ref, ybuf, slot, out_sem.at[slot], True).wait()


def _experts(block_e, n_used, n_real, src, dst, xs, l, w1, b1, w2, b2, n_tiles):
    d = D_MODEL
    f2 = w1.shape[-1]
    ys_rows = n_tiles * TILE_ROWS + EXPERT_BLOCK
    expert_block = lambda i, be, nb, nr, s, t: (l, be[i], 0, 0)
    grid_spec = pltpu.PrefetchScalarGridSpec(
        num_scalar_prefetch=5,
        grid=(N_BLOCKS,),
        in_specs=[
            pl.BlockSpec(memory_space=pl.ANY),
            pl.BlockSpec((1, 1, d, f2), expert_block),
            pl.BlockSpec((1, 1, 1, f2), expert_block),
            pl.BlockSpec((1, 1, D_FF, d), expert_block),
            pl.BlockSpec((1, 1, 1, d), expert_block),
        ],
        out_specs=pl.BlockSpec(memory_space=pl.ANY),
        scratch_shapes=[
            pltpu.VMEM((d, f2), BF16),
            pltpu.VMEM((D_FF, d), BF16),
            pltpu.VMEM((2, EXPERT_BLOCK, d + LANES), F32),
            pltpu.VMEM((2, EXPERT_BLOCK, d), F32),
            pltpu.SemaphoreType.DMA((2,)),
            pltpu.SemaphoreType.DMA((2,)),
        ],
    )
    return pl.pallas_call(
        _expert_kernel,
        grid_spec=grid_spec,
        out_shape=jax.ShapeDtypeStruct((ys_rows, d), F32),
        compiler_params=pltpu.CompilerParams(
            dimension_semantics=("arbitrary",), vmem_limit_bytes=VMEM_LIMIT),
        name="experts",
    )(block_e, n_used, n_real, src, dst, xs, w1, b1, w2, b2)


def _combine_kernel(ys_ref, meta_ref, x_ref, mod_ref, g_ref, b_ref, o_ref):
    tt = x_ref.shape[0]
    slot_iota = lax.broadcasted_iota(jnp.int32, (tt, TILE_ROWS), 1).astype(F32)
    p = jnp.zeros((tt, TILE_ROWS), F32)
    for k in range(TOP_K):
        p = p + (slot_iota == meta_ref[:, k:k + 1]).astype(F32)
    y = jnp.dot(p.astype(BF16), ys_ref[...].astype(BF16), preferred_element_type=F32)
    g2 = mod_ref[0, 0, 5:6, :]
    o_ref[...] = _ln(DEEPNORM_ALPHA * x_ref[...] + g2 * y) * g_ref[...] + b_ref[...]


def _combine(ys, meta, x1, mod, l, g, b, seq):
    n_tok, d = x1.shape
    tt = TOK_TILE
    tiles_per_seq = seq // tt
    return pl.pallas_call(
        _combine_kernel,
        grid=(n_tok // tt,),
        in_specs=[
            pl.BlockSpec((TILE_ROWS, d), lambda i: (i, 0)),
            pl.BlockSpec((tt, LANES), lambda i: (i, 0)),
            pl.BlockSpec((tt, d), lambda i: (i, 0)),
            pl.BlockSpec((1, 1, 6, d), lambda i: (l, i // tiles_per_seq, 0, 0)),
            pl.BlockSpec((1, d), lambda i: (0, 0)),
            pl.BlockSpec((1, d), lambda i: (0, 0)),
        ],
        out_specs=pl.BlockSpec((tt, d), lambda i: (i, 0)),
        out_shape=jax.ShapeDtypeStruct((n_tok, d), F32),
        compiler_params=pltpu.CompilerParams(
            dimension_semantics=("arbitrary",), vmem_limit_bytes=VMEM_LIMIT),
        name="combine",
    )(ys, meta, x1, mod, g, b)


def kernel(x, c, ada_w, ada_b, w_in, b_in, conv_a_w, conv_a_b, ln_a_g, ln_a_b, conv_b_w, w_pa, b_pa,
           w_pb, w_o, ln1_g, ln1_b, router_w, router_b, w1, b1, w2, b2, ln2_g, ln2_b):
    bsz, seq, d = x.shape
    n_layers = ada_w.shape[0]
    n_tok = bsz * seq
    n_tiles = n_tok // TOK_TILE
    assert (n_tok, d) == (N_TOKENS, D_MODEL)

    mod = _ada_mod(c, ada_w, ada_b)

    w_in_b = w_in.astype(BF16)
    w_pa_b = w_pa.astype(BF16)
    w_pb_b = w_pb.astype(BF16)
    w_o_b = w_o.astype(BF16)
    caw = jnp.repeat(conv_a_w, SUBLANES, axis=1)
    cbw = jnp.repeat(conv_b_w, SUBLANES, axis=1)
    rw_pad = jnp.pad(router_w, ((0, 0), (0, 0), (0, LANES - N_EXPERTS)))
    rw_hi = rw_pad.astype(BF16)
    rw_lo = (rw_pad - rw_hi.astype(F32)).astype(BF16)
    rb_pad = jnp.pad(router_b, ((0, 0), (0, LANES - N_EXPERTS)))[:, None, :]
    b1r = b1.reshape(n_layers, N_EXPERTS, 1, 2 * D_FF)
    b2r = b2.reshape(n_layers, N_EXPERTS, 1, d)
    row = lambda a, l: a[l][None, :]
    rows = lambda a: a[:, None, :]

    for l in range(n_layers):
        x = _token_mixer(x, mod, l, w_in_b, rows(b_in), caw, rows(conv_a_b), rows(ln_a_g), rows(ln_a_b),
                         cbw, w_pa_b, rows(b_pa), w_pb_b, w_o_b, rows(ln1_g), rows(ln1_b))
        x1 = x.reshape(n_tok, d)
        xs, meta, tab = _router(x1, mod, l, rw_hi, rw_lo, rb_pad, seq)
        block_e, n_used, n_real, src, dst = _granule_plan(tab, n_tiles)
        ys = _experts(block_e, n_used, n_real, src, dst, xs, l, w1, b1r, w2, b2r, n_tiles)
        x = _combine(ys, meta, x1, mod, l, row(ln2_g, l), row(ln2_b, l), seq)
        x = x.reshape(bsz, seq, d)
    return x
```

```python
import functools

import jax
import jax.numpy as jnp
from jax import lax
from jax.experimental import pallas as pl
from jax.experimental.pallas import tpu as pltpu

D_MODEL = 1024
DEPTH = 4
CONV_A = 31
CONV_B = 3
N_EXPERTS = 32
TOP_K = 4
D_FF = D_MODEL
SWIGLU_LIMIT = 7.0
SWIGLU_ALPHA = 1.702
EXPERT_BLOCK = 256
DEEPNORM_ALPHA = (2.0 * DEPTH) ** 0.25
LN_EPS = 1e-5

LANES = 128
SUBLANES = 8
VMEM_LIMIT = 56 * 1024 * 1024

MIX_TS = 256
CONV_RC = 64
CONV_LC = 256
PROJ_NC = 512
CARRY_A = 32
CARRY_B = 8
TOK_TILE = 256
GRANULE = SUBLANES
BLOCK_GRANULES = EXPERT_BLOCK // GRANULE
N_TOKENS = 16384
TILE_ROWS = 1280
assert TILE_ROWS >= TOP_K * TOK_TILE + N_EXPERTS * (GRANULE - 1) and TILE_ROWS % LANES == 0
N_BLOCKS = -(-((N_TOKENS // TOK_TILE) * (TILE_ROWS // GRANULE)
               + (N_EXPERTS + 1) * (BLOCK_GRANULES - 1)) // BLOCK_GRANULES)

F32 = jnp.float32
BF16 = jnp.bfloat16


def _ln(x):
    mu = jnp.mean(x, axis=-1, keepdims=True)
    xc = x - mu
    var = jnp.mean(xc * xc, axis=-1, keepdims=True)
    return xc * lax.rsqrt(var + LN_EPS)


NEG_LOG2E = -1.4426950408889634


def _sigmoid(x, scale=1.0):
    return 1.0 / (1.0 + jnp.exp2(x * (scale * NEG_LOG2E)))


def _bdot(a, b):
    return jnp.dot(a.astype(BF16), b, preferred_element_type=F32)


def _ada_kernel(c_ref, w_ref, b_ref, o_ref):
    c = c_ref[...]
    cond = c * _sigmoid(c)
    o_ref[0] = jnp.dot(cond, w_ref[0], preferred_element_type=F32,
                       precision=lax.Precision.HIGHEST) + b_ref[0]


def _ada_mod(c, ada_w, ada_b):
    bsz, d = c.shape
    n_layers = ada_w.shape[0]
    c_pad = jnp.zeros((SUBLANES, d), F32).at[:bsz].set(c)
    out = pl.pallas_call(
        _ada_kernel,
        grid=(n_layers, 6),
        in_specs=[
            pl.BlockSpec((SUBLANES, d), lambda l, j: (0, 0)),
            pl.BlockSpec((1, d, d), lambda l, j: (l, 0, j)),
            pl.BlockSpec((1, 1, d), lambda l, j: (l, 0, j)),
        ],
        out_specs=pl.BlockSpec((1, SUBLANES, d), lambda l, j: (l, 0, j)),
        out_shape=jax.ShapeDtypeStruct((n_layers, SUBLANES, 6 * d), F32),
        compiler_params=pltpu.CompilerParams(
            dimension_semantics=("arbitrary", "arbitrary"), vmem_limit_bytes=VMEM_LIMIT),
        name="ada_mod",
    )(c_pad, ada_w, ada_b.reshape(n_layers, 1, 6 * d))
    return out[:, :bsz].reshape(n_layers, bsz, 6, d)


def _conv_chunk(ext_ref, w_ref, out_ref, n_taps, carry_rows, r0, c0, dep=None):
    base = carry_rows - n_taps + 1
    out = None
    for b in range(SUBLANES):
        win = CONV_RC if b == 0 else CONV_RC + SUBLANES
        z = None
        for k in range(n_taps):
            if (base + k) % SUBLANES != b:
                continue
            a8 = base + k - b
            w8 = w_ref[pl.ds(SUBLANES * k, SUBLANES), c0:c0 + CONV_LC]
            term = jnp.tile(w8, (win // SUBLANES, 1)) * ext_ref[pl.ds(r0 + a8, win), c0:c0 + CONV_LC]
            z = term if z is None else z + term
        if z is None:
            continue
        if b != 0:
            z = pltpu.roll(z, win - b, axis=0)[0:CONV_RC]
        out = z if out is None else out + z
    if dep is not None:
        out = out + jnp.tile(dep, (CONV_RC // SUBLANES, CONV_LC // LANES))
    out_ref[pl.ds(r0, CONV_RC), c0:c0 + CONV_LC] = out


def _mixer_kernel(x_ref, mod_ref, w_in_ref, b_in_ref, caw_ref, cab_ref, lag_ref, lab_ref,
                  cbw_ref, w_pa_ref, b_pa_ref, w_pb_ref, w_o_ref, g_ref, b_ref,
                  o_ref, ext_a, ext_b, cv_a, cv_b, gb_s, sg_s):
    ts = x_ref.shape[1]
    d = D_MODEL

    @pl.when(pl.program_id(1) == 0)
    def _():
        ext_a[pl.ds(0, CARRY_A), :] = jnp.zeros((CARRY_A, d), F32)
        ext_b[pl.ds(0, CARRY_B), :] = jnp.zeros((CARRY_B, d), F32)

    xt = x_ref[0]
    sh1 = mod_ref[0, 0, 0:1, :]
    sc1 = mod_ref[0, 0, 1:2, :]
    g1 = mod_ref[0, 0, 2:3, :]
    u = (_ln(xt) * (1.0 + sc1) + sh1).astype(BF16)

    def proj(c0, n):
        return jnp.dot(u, w_in_ref[:, c0:c0 + n], preferred_element_type=F32) + b_in_ref[:, c0:c0 + n]

    za = proj(0, 2 * d)
    ext_a[pl.ds(CARRY_A, ts), :] = za[:, 0:d] * _sigmoid(za[:, d:2 * d])

    cur = pl.ds(CARRY_B, ts)

    def tie(z):
        bits = lax.bitcast_convert_type(z[0:SUBLANES, 0:LANES], jnp.uint32)
        zero = lax.shift_right_logical(lax.shift_right_logical(bits, jnp.uint32(16)), jnp.uint32(16))
        return lax.bitcast_convert_type(zero, F32)

    def mixer_b_chunk(c):
        z = proj(2 * d + c, PROJ_NC)
        if c < d:
            gb_s[:, c:c + PROJ_NC] = z
        elif c < 2 * d:
            ext_b[cur, c - d:c - d + PROJ_NC] = z
        else:
            ext_b[cur, c - 2 * d:c - 2 * d + PROJ_NC] = ext_b[cur, c - 2 * d:c - 2 * d + PROJ_NC] * z
        return tie(z)

    def gate_chunk(c):
        z = proj(5 * d + c, PROJ_NC)
        sg_s[:, c:c + PROJ_NC] = _sigmoid(z)
        return tie(z)

    side = [lambda c=c: mixer_b_chunk(c) for c in range(0, 3 * d, PROJ_NC)]
    side += [lambda c=c: gate_chunk(c) for c in range(0, 2 * d, PROJ_NC)]
    chunks = [(r0, c0) for r0 in range(0, ts, CONV_RC) for c0 in range(0, d, CONV_LC)]
    assert len(side) < len(chunks)
    dep = None
    for i, (r0, c0) in enumerate(chunks):
        nxt = side[i]() if i < len(side) else None
        _conv_chunk(ext_a, caw_ref, cv_a, CONV_A, CARRY_A, r0, c0, dep)
        dep = nxt
    for r0, c0 in chunks:
        _conv_chunk(ext_b, cbw_ref, cv_b, CONV_B, CARRY_B, r0, c0)

    ext_a[pl.ds(0, CARRY_A), :] = ext_a[pl.ds(ts, CARRY_A), :]
    ext_b[pl.ds(0, CARRY_B), :] = ext_b[pl.ds(ts, CARRY_B), :]

    ya = _ln(cv_a[...] + cab_ref[...]) * lag_ref[...] + lab_ref[...]
    ya = ya * _sigmoid(ya)
    ya = _bdot(ya, w_pa_ref[...]) + b_pa_ref[...]
    yb = _bdot(gb_s[...] * cv_b[...], w_pb_ref[...])

    m = sg_s[:, 0:d] * ya + sg_s[:, d:2 * d] * yb
    y = _bdot(m, w_o_ref[...])
    o_ref[0] = _ln(DEEPNORM_ALPHA * xt + g1 * y) * g_ref[...] + b_ref[...]


def _layer_spec(l, shape):
    nd = len(shape)
    return pl.BlockSpec((None,) + shape, lambda b, s: (l,) + (0,) * nd)


def _token_mixer(x, mod, l, w_in, b_in, caw, cab, lag, lab, cbw, w_pa, b_pa, w_pb, w_o, g, b):
    bsz, seq, d = x.shape
    ts = MIX_TS
    n_in = w_in.shape[-1]
    return pl.pallas_call(
        _mixer_kernel,
        grid=(bsz, seq // ts),
        in_specs=[
            pl.BlockSpec((1, ts, d), lambda b, s: (b, s, 0)),
            pl.BlockSpec((1, 1, 6, d), lambda b, s: (l, b, 0, 0)),
            _layer_spec(l, (d, n_in)),
            _layer_spec(l, (1, n_in)),
            _layer_spec(l, (SUBLANES * CONV_A, d)),
            _layer_spec(l, (1, d)),
            _layer_spec(l, (1, d)),
            _layer_spec(l, (1, d)),
            _layer_spec(l, (SUBLANES * CONV_B, d)),
            _layer_spec(l, (d, d)),
            _layer_spec(l, (1, d)),
            _layer_spec(l, (d, d)),
            _layer_spec(l, (d, d)),
            _layer_spec(l, (1, d)),
            _layer_spec(l, (1, d)),
        ],
        out_specs=pl.BlockSpec((1, ts, d), lambda b, s: (b, s, 0)),
        out_shape=jax.ShapeDtypeStruct((bsz, seq, d), F32),
        scratch_shapes=[
            pltpu.VMEM((ts + CARRY_A, d), F32),
            pltpu.VMEM((ts + CARRY_B, d), F32),
            pltpu.VMEM((ts, d), F32),
            pltpu.VMEM((ts, d), F32),
            pltpu.VMEM((ts, d), F32),
            pltpu.VMEM((ts, 2 * d), F32),
        ],
        compiler_params=pltpu.CompilerParams(
            dimension_semantics=("arbitrary", "arbitrary"), vmem_limit_bytes=VMEM_LIMIT),
        name="token_mixer",
    )(x, mod, w_in, b_in, caw, cab, lag, lab, cbw, w_pa, b_pa, w_pb, w_o, g, b)


def _router_kernel(x_ref, mod_ref, rwh_ref, rwl_ref, rb_ref, xs_ref, meta_ref, tab_ref):
    tt = x_ref.shape[0]
    d = D_MODEL

    sh2 = mod_ref[0, 0, 3:4, :]
    sc2 = mod_ref[0, 0, 4:5, :]
    u2 = _ln(x_ref[...]) * (1.0 + sc2) + sh2
    u2h = u2.astype(BF16)
    u2l = (u2 - u2h.astype(F32)).astype(BF16)
    logits = (jnp.dot(u2h, rwh_ref[...], preferred_element_type=F32)
              + jnp.dot(u2l, rwh_ref[...], preferred_element_type=F32)
              + jnp.dot(u2h, rwl_ref[...], preferred_element_type=F32)) + rb_ref[...]
    lane = lax.broadcasted_iota(jnp.int32, (tt, LANES), 1)
    lane_f = lane.astype(F32)
    neg_inf = jnp.float32(-jnp.inf)
    work = jnp.where(lane < N_EXPERTS, logits, neg_inf)
    sel = jnp.zeros((tt, LANES), F32)
    vals, idxs = [], []
    for _ in range(TOP_K):
        m = jnp.max(work, axis=-1, keepdims=True)
        idx = jnp.min(jnp.where(work == m, lane_f, float(LANES)), axis=-1, keepdims=True)
        hit = lane_f == idx
        vals.append(m)
        idxs.append(idx)
        work = jnp.where(hit, neg_inf, work)
        sel = sel + hit.astype(F32)
    exps = [jnp.exp(v - vals[0]) for v in vals]
    inv = 1.0 / (exps[0] + exps[1] + exps[2] + exps[3])

    n_e = jnp.sum(sel, axis=0, keepdims=True)
    c8 = jnp.floor((n_e + float(GRANULE - 1)) * (1.0 / GRANULE))
    er = lax.broadcasted_iota(jnp.int32, (LANES, LANES), 0)
    ec = lax.broadcasted_iota(jnp.int32, (LANES, LANES), 1)
    upper = (er < ec).astype(BF16)
    off8 = jnp.dot(jnp.broadcast_to(c8, (SUBLANES, LANES)).astype(BF16), upper,
                   preferred_element_type=F32)[0:1]
    row = lax.broadcasted_iota(jnp.int32, (tt, tt), 0)
    col = lax.broadcasted_iota(jnp.int32, (tt, tt), 1)
    tri = (col < row).astype(BF16)
    prefix = jnp.dot(tri, sel.astype(BF16), preferred_element_type=F32)
    slot_dense = off8 * float(GRANULE) + prefix

    meta = jnp.zeros((tt, LANES), F32)
    for k in range(TOP_K):
        pos_k = jnp.sum(jnp.where(lane_f == idxs[k], slot_dense, 0.0), axis=-1, keepdims=True)
        meta = jnp.where(lane == k, pos_k, meta)
        meta = jnp.where(lane == TOP_K + k, exps[k] * inv, meta)
    meta_ref[...] = meta

    sub = lax.broadcasted_iota(jnp.int32, (SUBLANES, LANES), 0)
    tab_ref[0] = jnp.where(sub == 0, c8, jnp.where(sub == 1, off8, 0.0))

    meta_t = meta.T
    slot_iota = lax.broadcasted_iota(jnp.int32, (TILE_ROWS, tt), 0).astype(F32)
    pt = jnp.zeros((TILE_ROWS, tt), F32)
    gm = jnp.zeros((TILE_ROWS, tt), F32)
    for k in range(TOP_K):
        hit = slot_iota == meta_t[k:k + 1, :]
        pt = pt + hit.astype(F32)
        gm = gm + jnp.where(hit, meta_t[TOP_K + k:TOP_K + k + 1, :], 0.0)
    xs_ref[:, 0:d] = jnp.dot(pt.astype(BF16), u2.astype(BF16), preferred_element_type=F32)
    gate_row = jnp.sum(gm, axis=-1, keepdims=True)
    xs_ref[:, d:d + LANES] = jnp.broadcast_to(gate_row, (TILE_ROWS, LANES))


def _router(x1, mod, l, rw_hi, rw_lo, rb_pad, seq):
    n_tok, d = x1.shape
    tt = TOK_TILE
    n_tiles = n_tok // tt
    tiles_per_seq = seq // tt
    return pl.pallas_call(
        _router_kernel,
        grid=(n_tiles,),
        in_specs=[
            pl.BlockSpec((tt, d), lambda i: (i, 0)),
            pl.BlockSpec((1, 1, 6, d), lambda i: (l, i // tiles_per_seq, 0, 0)),
            pl.BlockSpec((None, d, LANES), lambda i: (l, 0, 0)),
            pl.BlockSpec((None, d, LANES), lambda i: (l, 0, 0)),
            pl.BlockSpec((None, 1, LANES), lambda i: (l, 0, 0)),
        ],
        out_specs=[
            pl.BlockSpec((TILE_ROWS, d + LANES), lambda i: (i, 0)),
            pl.BlockSpec((tt, LANES), lambda i: (i, 0)),
            pl.BlockSpec((1, SUBLANES, LANES), lambda i: (i, 0, 0)),
        ],
        out_shape=[
            jax.ShapeDtypeStruct((n_tiles * TILE_ROWS, d + LANES), F32),
            jax.ShapeDtypeStruct((n_tok, LANES), F32),
            jax.ShapeDtypeStruct((n_tiles, SUBLANES, LANES), F32),
        ],
        compiler_params=pltpu.CompilerParams(
            dimension_semantics=("arbitrary",), vmem_limit_bytes=VMEM_LIMIT),
        name="router",
    )(x1, mod, rw_hi, rw_lo, rb_pad)


def _granule_plan(tab, n_tiles):
    c8 = tab[:, 0, :N_EXPERTS].astype(jnp.int32).T
    off8 = tab[:, 1, :N_EXPERTS].astype(jnp.int32).T
    used = jnp.sum(c8, axis=0, keepdims=True)
    c8 = jnp.concatenate([c8, TILE_ROWS // GRANULE - used], axis=0)
    off8 = jnp.concatenate([off8, used], axis=0)
    per_e = jnp.sum(c8, axis=1)
    padded = (per_e + BLOCK_GRANULES - 1) // BLOCK_GRANULES * BLOCK_GRANULES
    e_end = jnp.cumsum(padded)
    e_start = e_end - padded
    strip_start = (e_start[:, None] + jnp.cumsum(c8, axis=1) - c8).reshape(-1)
    cnt_flat = c8.reshape(-1)
    off_flat = off8.reshape(-1)
    slot = jnp.arange(N_BLOCKS * BLOCK_GRANULES, dtype=jnp.int32)
    before = strip_start[None, :] <= slot[:, None]
    tile_flat = jnp.arange(c8.size, dtype=jnp.int32) % n_tiles

    def at_strip(table):
        diff = table - jnp.concatenate([jnp.zeros((1,), jnp.int32), table[:-1]])
        return jnp.sum(jnp.where(before, diff[None, :], 0), axis=1)

    g = slot - at_strip(strip_start)
    valid = g < at_strip(cnt_flat)
    rows = at_strip(tile_flat * TILE_ROWS + off_flat * GRANULE) + g * GRANULE
    src = jnp.where(valid, rows, 0)
    trash = n_tiles * TILE_ROWS + (slot % BLOCK_GRANULES) * GRANULE
    dst = jnp.where(valid, rows, trash)
    block_start = jnp.arange(N_BLOCKS, dtype=jnp.int32) * BLOCK_GRANULES
    block_e = jnp.minimum(
        jnp.sum((e_end[None, :] <= block_start[:, None]).astype(jnp.int32), axis=1), N_EXPERTS - 1)
    n_used = e_end[-1:] // BLOCK_GRANULES
    n_real = e_end[N_EXPERTS - 1:N_EXPERTS] // BLOCK_GRANULES
    end_blk = e_end[:N_EXPERTS] // BLOCK_GRANULES
    expert_ids = jnp.arange(N_EXPERTS, dtype=jnp.int32)
    run_end = jnp.sum(jnp.where(block_e[:, None] == expert_ids[None, :], end_blk[None, :], 0), axis=1)
    after = jnp.minimum(jnp.sum((end_blk[None, :] <= run_end[:, None]).astype(jnp.int32), axis=1), N_EXPERTS - 1)
    next_e = jnp.where(run_end < n_real, after, -1)
    i32 = lambda a: a.astype(jnp.int32)
    return i32(block_e), i32(next_e), i32(n_used), i32(n_real), i32(src), i32(dst)


def _granule_copy(hbm_ref, row, buf_ref, slot, i, sem, to_hbm):
    hbm = hbm_ref.at[pl.ds(pl.multiple_of(row, GRANULE), GRANULE)]
    vmem = buf_ref.at[slot, pl.ds(i * GRANULE, GRANULE)]
    return pltpu.make_async_copy(vmem, hbm, sem) if to_hbm else pltpu.make_async_copy(hbm, vmem, sem)


def _block_copy(hbm_ref, buf_ref, slot, sem, to_hbm):
    hbm = hbm_ref.at[pl.ds(0, EXPERT_BLOCK)]
    vmem = buf_ref.at[slot]
    return pltpu.make_async_copy(vmem, hbm, sem) if to_hbm else pltpu.make_async_copy(hbm, vmem, sem)


def _expert_kernel(layer, be_ref, ne_ref, nb_ref, nr_ref, src_ref, dst_ref, xs_ref, w1_ref, b1_ref, w2_ref,
                   b2_ref, ys_ref, w1s, w2s, w1b, w2b, xbuf, ybuf, in_sem, out_sem, w_sem):
    b = pl.program_id(0)
    n_used = nb_ref[0]
    n_real = nr_ref[0]
    e = be_ref[b]
    prev = be_ref[jnp.maximum(b - 1, 0)]
    slot = b % 2

    def start_gather(blk, s):
        for i in range(BLOCK_GRANULES):
            _granule_copy(xs_ref, src_ref[blk * BLOCK_GRANULES + i], xbuf, s, i, in_sem.at[s], False).start()

    def weight_copies(expert):
        return (pltpu.make_async_copy(w1_ref.at[layer, expert], w1s, w_sem.at[0]),
                pltpu.make_async_copy(w2_ref.at[layer, expert], w2s, w_sem.at[1]))

    @pl.when(b == 0)
    def _():
        for cp in weight_copies(e):
            cp.start()

    @pl.when(jnp.logical_and(b < n_real, jnp.logical_or(b == 0, e != prev)))
    def _():
        for cp in weight_copies(e):
            cp.wait()
        w1b[...] = w1s[...].astype(BF16)
        w2b[...] = w2s[...].astype(BF16)

        @pl.when(ne_ref[b] >= 0)
        def _():
            for cp in weight_copies(ne_ref[b]):
                cp.start()

    @pl.when(b == 0)
    def _():
        start_gather(0, 0)
        ybuf[1] = jnp.zeros(ybuf.shape[1:], F32)
        pad_rows = ys_ref.at[pl.ds(ys_ref.shape[0] - EXPERT_BLOCK, EXPERT_BLOCK)]
        fill = pltpu.make_async_copy(ybuf.at[1], pad_rows, out_sem.at[1])
        fill.start()
        fill.wait()

    @pl.when(b + 1 < n_used)
    def _():
        start_gather(b + 1, 1 - slot)

    @pl.when(b < n_used)
    def _():
        _block_copy(xs_ref, xbuf, slot, in_sem.at[slot], False).wait()

        @pl.when(b >= 2)
        def _():
            _block_copy(ys_ref, ybuf, slot, out_sem.at[slot], True).wait()

        @pl.when(b < n_real)
        def _():
            xb = xbuf[slot]
            h = _bdot(xb[:, 0:D_MODEL], w1b[...]) + b1_ref[0, 0]
            h_glu = jnp.minimum(h[:, 0:D_FF], SWIGLU_LIMIT)
            h_lin = jnp.clip(h[:, D_FF:2 * D_FF], -SWIGLU_LIMIT, SWIGLU_LIMIT)
            a = h_glu * _sigmoid(h_glu, SWIGLU_ALPHA) * (h_lin + 1.0)
            ybuf[slot] = (_bdot(a, w2b[...]) + b2_ref[0, 0]) * xb[:, D_MODEL:D_MODEL + 1]

        @pl.when(b >= n_real)
        def _():
            ybuf[slot] = jnp.zeros(ybuf.shape[1:], F32)

        for i in range(BLOCK_GRANULES):
            _granule_copy(ys_ref, dst_ref[b * BLOCK_GRANULES + i], ybuf, slot, i, out_sem.at[slot], True).start()

    @pl.when(b == n_used - 1)
    def _():
        @pl.when(b >= 1)
        def _():
            _block_copy(ys_ref, ybuf, 1 - slot, out_sem.at[1 - slot], True).wait()

        _block_copy(ys_ref, ybuf, slot, out_sem.at[slot], True).wait()


def _experts(block_e, next_e, n_used, n_real, src, dst, xs, l, w1, b1, w2, b2, n_tiles):
    d = D_MODEL
    f2 = w1.shape[-1]
    ys_rows = n_tiles * TILE_ROWS + EXPERT_BLOCK
    expert_block = lambda i, be, ne, nb, nr, s, t: (l, be[i], 0, 0)
    grid_spec = pltpu.PrefetchScalarGridSpec(
        num_scalar_prefetch=6,
        grid=(N_BLOCKS,),
        in_specs=[
            pl.BlockSpec(memory_space=pl.ANY),
            pl.BlockSpec(memory_space=pl.ANY),
            pl.BlockSpec((1, 1, 1, f2), expert_block),
            pl.BlockSpec(memory_space=pl.ANY),
            pl.BlockSpec((1, 1, 1, d), expert_block),
        ],
        out_specs=pl.BlockSpec(memory_space=pl.ANY),
        scratch_shapes=[
            pltpu.VMEM((d, f2), F32),
            pltpu.VMEM((D_FF, d), F32),
            pltpu.VMEM((d, f2), BF16),
            pltpu.VMEM((D_FF, d), BF16),
            pltpu.VMEM((2, EXPERT_BLOCK, d + LANES), F32),
            pltpu.VMEM((2, EXPERT_BLOCK, d), F32),
            pltpu.SemaphoreType.DMA((2,)),
            pltpu.SemaphoreType.DMA((2,)),
            pltpu.SemaphoreType.DMA((2,)),
        ],
    )
    return pl.pallas_call(
        functools.partial(_expert_kernel, l),
        grid_spec=grid_spec,
        out_shape=jax.ShapeDtypeStruct((ys_rows, d), F32),
        compiler_params=pltpu.CompilerParams(
            dimension_semantics=("arbitrary",), vmem_limit_bytes=VMEM_LIMIT),
        name="experts",
    )(block_e, next_e, n_used, n_real, src, dst, xs, w1, b1, w2, b2)


def _combine_kernel(ys_ref, meta_ref, x_ref, mod_ref, g_ref, b_ref, o_ref):
    tt = x_ref.shape[0]
    slot_iota = lax.broadcasted_iota(jnp.int32, (tt, TILE_ROWS), 1).astype(F32)
    p = jnp.zeros((tt, TILE_ROWS), F32)
    for k in range(TOP_K):
        p = p + (slot_iota == meta_ref[:, k:k + 1]).astype(F32)
    y = jnp.dot(p.astype(BF16), ys_ref[...].astype(BF16), preferred_element_type=F32)
    g2 = mod_ref[0, 0, 5:6, :]
    o_ref[...] = _ln(DEEPNORM_ALPHA * x_ref[...] + g2 * y) * g_ref[...] + b_ref[...]


def _combine(ys, meta, x1, mod, l, g, b, seq):
    n_tok, d = x1.shape
    tt = TOK_TILE
    tiles_per_seq = seq // tt
    return pl.pallas_call(
        _combine_kernel,
        grid=(n_tok // tt,),
        in_specs=[
            pl.BlockSpec((TILE_ROWS, d), lambda i: (i, 0)),
            pl.BlockSpec((tt, LANES), lambda i: (i, 0)),
            pl.BlockSpec((tt, d), lambda i: (i, 0)),
            pl.BlockSpec((1, 1, 6, d), lambda i: (l, i // tiles_per_seq, 0, 0)),
            pl.BlockSpec((1, d), lambda i: (0, 0)),
            pl.BlockSpec((1, d), lambda i: (0, 0)),
        ],
        out_specs=pl.BlockSpec((tt, d), lambda i: (i, 0)),
        out_shape=jax.ShapeDtypeStruct((n_tok, d), F32),
        compiler_params=pltpu.CompilerParams(
            dimension_semantics=("arbitrary",), vmem_limit_bytes=VMEM_LIMIT),
        name="combine",
    )(ys, meta, x1, mod, g, b)


def kernel(x, c, ada_w, ada_b, w_in, b_in, conv_a_w, conv_a_b, ln_a_g, ln_a_b, conv_b_w, w_pa, b_pa,
           w_pb, w_o, ln1_g, ln1_b, router_w, router_b, w1, b1, w2, b2, ln2_g, ln2_b):
    bsz, seq, d = x.shape
    n_layers = ada_w.shape[0]
    n_tok = bsz * seq
    n_tiles = n_tok // TOK_TILE
    assert (n_tok, d) == (N_TOKENS, D_MODEL)

    mod = _ada_mod(c, ada_w, ada_b)

    w_in_b = w_in.astype(BF16)
    w_pa_b = w_pa.astype(BF16)
    w_pb_b = w_pb.astype(BF16)
    w_o_b = w_o.astype(BF16)
    caw = jnp.repeat(conv_a_w, SUBLANES, axis=1)
    cbw = jnp.repeat(conv_b_w, SUBLANES, axis=1)
    rw_pad = jnp.pad(router_w, ((0, 0), (0, 0), (0, LANES - N_EXPERTS)))
    rw_hi = rw_pad.astype(BF16)
    rw_lo = (rw_pad - rw_hi.astype(F32)).astype(BF16)
    rb_pad = jnp.pad(router_b, ((0, 0), (0, LANES - N_EXPERTS)))[:, None, :]
    b1r = b1.reshape(n_layers, N_EXPERTS, 1, 2 * D_FF)
    b2r = b2.reshape(n_layers, N_EXPERTS, 1, d)
    row = lambda a, l: a[l][None, :]
    rows = lambda a: a[:, None, :]

    for l in range(n_layers):
        x = _token_mixer(x, mod, l, w_in_b, rows(b_in), caw, rows(conv_a_b), rows(ln_a_g), rows(ln_a_b),
                         cbw, w_pa_b, rows(b_pa), w_pb_b, w_o_b, rows(ln1_g), rows(ln1_b))
        x1 = x.reshape(n_tok, d)
        xs, meta, tab = _router(x1, mod, l, rw_hi, rw_lo, rb_pad, seq)
        block_e, next_e, n_used, n_real, src, dst = _granule_plan(tab, n_tiles)
        ys = _experts(block_e, next_e, n_used, n_real, src, dst, xs, l, w1, b1r, w2, b2r, n_tiles)
        x = _combine(ys, meta, x1, mod, l, row(ln2_g, l), row(ln2_b, l), seq)
        x = x.reshape(bsz, seq, d)
    return x
```

```python
import functools

import jax
import jax.numpy as jnp
from jax import lax
from jax.experimental import pallas as pl
from jax.experimental.pallas import tpu as pltpu

D_MODEL = 1024
DEPTH = 4
CONV_A = 31
CONV_B = 3
N_EXPERTS = 32
TOP_K = 4
D_FF = D_MODEL
SWIGLU_LIMIT = 7.0
SWIGLU_ALPHA = 1.702
EXPERT_BLOCK = 256
DEEPNORM_ALPHA = (2.0 * DEPTH) ** 0.25
LN_EPS = 1e-5

LANES = 128
SUBLANES = 8
VMEM_LIMIT = 56 * 1024 * 1024

MIX_TS = 256
CONV_RC = 64
CONV_LC = 256
PROJ_NC = 512
CARRY_A = 32
CARRY_B = 8
TOK_TILE = 256
GRANULE = SUBLANES
BLOCK_GRANULES = EXPERT_BLOCK // GRANULE
N_TOKENS = 16384
TILE_ROWS = 1280
HALF_D = D_MODEL // 2
XS_WORDS = HALF_D + LANES
assert TILE_ROWS >= TOP_K * TOK_TILE + N_EXPERTS * (GRANULE - 1) and TILE_ROWS % LANES == 0
N_BLOCKS = -(-((N_TOKENS // TOK_TILE) * (TILE_ROWS // GRANULE)
               + (N_EXPERTS + 1) * (BLOCK_GRANULES - 1)) // BLOCK_GRANULES)

F32 = jnp.float32
BF16 = jnp.bfloat16


def _ln(x):
    mu = jnp.mean(x, axis=-1, keepdims=True)
    xc = x - mu
    var = jnp.mean(xc * xc, axis=-1, keepdims=True)
    return xc * lax.rsqrt(var + LN_EPS)


NEG_LOG2E = -1.4426950408889634


def _sigmoid(x, scale=1.0):
    return 1.0 / (1.0 + jnp.exp2(x * (scale * NEG_LOG2E)))


def _bdot(a, b):
    return jnp.dot(a.astype(BF16), b, preferred_element_type=F32)


HIGH_HALF = 0xFFFF0000


def _pack_halves(x):
    h = x.shape[1] // 2
    lo = lax.shift_right_logical(lax.bitcast_convert_type(x[:, :h], jnp.uint32), jnp.uint32(16))
    hi = lax.bitcast_convert_type(x[:, h:], jnp.uint32) & jnp.uint32(HIGH_HALF)
    return hi | lo


def _unpack_halves(w):
    lo = lax.bitcast_convert_type(lax.shift_left(w, jnp.uint32(16)), F32)
    hi = lax.bitcast_convert_type(w & jnp.uint32(HIGH_HALF), F32)
    return jnp.concatenate([lo.astype(BF16), hi.astype(BF16)], axis=1)


def _ada_kernel(c_ref, w_ref, b_ref, o_ref):
    c = c_ref[...]
    cond = c * _sigmoid(c)
    o_ref[0] = jnp.dot(cond, w_ref[0], preferred_element_type=F32,
                       precision=lax.Precision.HIGHEST) + b_ref[0]


def _ada_mod(c, ada_w, ada_b):
    bsz, d = c.shape
    n_layers = ada_w.shape[0]
    c_pad = jnp.zeros((SUBLANES, d), F32).at[:bsz].set(c)
    out = pl.pallas_call(
        _ada_kernel,
        grid=(n_layers, 6),
        in_specs=[
            pl.BlockSpec((SUBLANES, d), lambda l, j: (0, 0)),
            pl.BlockSpec((1, d, d), lambda l, j: (l, 0, j)),
            pl.BlockSpec((1, 1, d), lambda l, j: (l, 0, j)),
        ],
        out_specs=pl.BlockSpec((1, SUBLANES, d), lambda l, j: (l, 0, j)),
        out_shape=jax.ShapeDtypeStruct((n_layers, SUBLANES, 6 * d), F32),
        compiler_params=pltpu.CompilerParams(
            dimension_semantics=("arbitrary", "arbitrary"), vmem_limit_bytes=VMEM_LIMIT),
        name="ada_mod",
    )(c_pad, ada_w, ada_b.reshape(n_layers, 1, 6 * d))
    return out[:, :bsz].reshape(n_layers, bsz, 6, d)


def _conv_chunk(ext_ref, w_ref, out_ref, n_taps, carry_rows, r0, c0, dep=None):
    base = carry_rows - n_taps + 1
    out = None
    for b in range(SUBLANES):
        win = CONV_RC if b == 0 else CONV_RC + SUBLANES
        z = None
        for k in range(n_taps):
            if (base + k) % SUBLANES != b:
                continue
            a8 = base + k - b
            w8 = w_ref[pl.ds(SUBLANES * k, SUBLANES), c0:c0 + CONV_LC]
            term = jnp.tile(w8, (win // SUBLANES, 1)) * ext_ref[pl.ds(r0 + a8, win), c0:c0 + CONV_LC]
            z = term if z is None else z + term
        if z is None:
            continue
        if b != 0:
            z = pltpu.roll(z, win - b, axis=0)[0:CONV_RC]
        out = z if out is None else out + z
    if dep is not None:
        out = out + jnp.tile(dep, (CONV_RC // SUBLANES, CONV_LC // LANES))
    out_ref[pl.ds(r0, CONV_RC), c0:c0 + CONV_LC] = out


def _mixer_kernel(x_ref, mod_ref, w_in_ref, b_in_ref, caw_ref, cab_ref, lag_ref, lab_ref,
                  cbw_ref, w_pa_ref, b_pa_ref, w_pb_ref, w_o_ref, g_ref, b_ref,
                  o_ref, ext_a, ext_b, cv_a, cv_b, gb_s, sg_s):
    ts = x_ref.shape[1]
    d = D_MODEL

    @pl.when(pl.program_id(1) == 0)
    def _():
        ext_a[pl.ds(0, CARRY_A), :] = jnp.zeros((CARRY_A, d), F32)
        ext_b[pl.ds(0, CARRY_B), :] = jnp.zeros((CARRY_B, d), F32)

    xt = x_ref[0]
    sh1 = mod_ref[0, 0, 0:1, :]
    sc1 = mod_ref[0, 0, 1:2, :]
    g1 = mod_ref[0, 0, 2:3, :]
    u = (_ln(xt) * (1.0 + sc1) + sh1).astype(BF16)

    def proj(c0, n):
        return jnp.dot(u, w_in_ref[:, c0:c0 + n], preferred_element_type=F32) + b_in_ref[:, c0:c0 + n]

    cur_a = pl.ds(CARRY_A, ts)
    cur_b = pl.ds(CARRY_B, ts)
    nc = PROJ_NC

    def tie(z):
        bits = lax.bitcast_convert_type(z[0:SUBLANES, 0:LANES], jnp.uint32)
        zero = lax.shift_right_logical(lax.shift_right_logical(bits, jnp.uint32(16)), jnp.uint32(16))
        return lax.bitcast_convert_type(zero, F32)

    def glu_value(c):
        z = proj(c, nc)
        ext_a[cur_a, c:c + nc] = z
        return tie(z)

    def glu_gate(c):
        z = proj(d + c, nc)
        ext_a[cur_a, c:c + nc] = ext_a[cur_a, c:c + nc] * _sigmoid(z)
        return tie(z)

    def b_gate(c):
        z = proj(2 * d + c, nc)
        gb_s[:, c:c + nc] = z
        return tie(z)

    def b_c(c):
        z = proj(3 * d + c, nc)
        ext_b[cur_b, c:c + nc] = z
        return tie(z)

    def b_h(c):
        z = proj(4 * d + c, nc)
        ext_b[cur_b, c:c + nc] = ext_b[cur_b, c:c + nc] * z
        return tie(z)

    def merge_gate(c):
        z = proj(5 * d + c, nc)
        sg_s[:, c:c + nc] = _sigmoid(z)
        return tie(z)

    glu_value(0)
    glu_gate(0)
    jobs = [lambda c=c, f=f: f(c) for c in range(nc, d, nc) for f in (glu_value, glu_gate)]
    jobs += [lambda c=c, f=f: f(c) for c in range(0, d, nc) for f in (b_c, b_h)]
    jobs += [lambda c=c: b_gate(c) for c in range(0, d, nc)]
    jobs += [lambda c=c: merge_gate(c) for c in range(0, 2 * d, nc)]
    conv_chunks = [(r0, c0) for c0 in range(0, d, CONV_LC) for r0 in range(0, ts, CONV_RC)]
    assert len(jobs) <= len(conv_chunks)
    job_at = {(i * len(conv_chunks)) // len(jobs): job for i, job in enumerate(jobs)}
    dep = None
    for i, (r0, c0) in enumerate(conv_chunks):
        nxt = job_at[i]() if i in job_at else None
        _conv_chunk(ext_a, caw_ref, cv_a, CONV_A, CARRY_A, r0, c0, dep)
        dep = nxt
    for r0, c0 in conv_chunks:
        _conv_chunk(ext_b, cbw_ref, cv_b, CONV_B, CARRY_B, r0, c0)

    ext_a[pl.ds(0, CARRY_A), :] = ext_a[pl.ds(ts, CARRY_A), :]
    ext_b[pl.ds(0, CARRY_B), :] = ext_b[pl.ds(ts, CARRY_B), :]

    ya = _ln(cv_a[...] + cab_ref[...]) * lag_ref[...] + lab_ref[...]
    ya = ya * _sigmoid(ya)
    ya = _bdot(ya, w_pa_ref[...]) + b_pa_ref[...]
    yb = _bdot(gb_s[...] * cv_b[...], w_pb_ref[...])

    m = sg_s[:, 0:d] * ya + sg_s[:, d:2 * d] * yb
    y = _bdot(m, w_o_ref[...])
    o_ref[0] = _ln(DEEPNORM_ALPHA * xt + g1 * y) * g_ref[...] + b_ref[...]


def _layer_spec(l, shape):
    nd = len(shape)
    return pl.BlockSpec((None,) + shape, lambda b, s: (l,) + (0,) * nd)


def _token_mixer(x, mod, l, w_in, b_in, caw, cab, lag, lab, cbw, w_pa, b_pa, w_pb, w_o, g, b):
    bsz, seq, d = x.shape
    ts = MIX_TS
    n_in = w_in.shape[-1]
    return pl.pallas_call(
        _mixer_kernel,
        grid=(bsz, seq // ts),
        in_specs=[
            pl.BlockSpec((1, ts, d), lambda b, s: (b, s, 0)),
            pl.BlockSpec((1, 1, 6, d), lambda b, s: (l, b, 0, 0)),
            _layer_spec(l, (d, n_in)),
            _layer_spec(l, (1, n_in)),
            _layer_spec(l, (SUBLANES * CONV_A, d)),
            _layer_spec(l, (1, d)),
            _layer_spec(l, (1, d)),
            _layer_spec(l, (1, d)),
            _layer_spec(l, (SUBLANES * CONV_B, d)),
            _layer_spec(l, (d, d)),
            _layer_spec(l, (1, d)),
            _layer_spec(l, (d, d)),
            _layer_spec(l, (d, d)),
            _layer_spec(l, (1, d)),
            _layer_spec(l, (1, d)),
        ],
        out_specs=pl.BlockSpec((1, ts, d), lambda b, s: (b, s, 0)),
        out_shape=jax.ShapeDtypeStruct((bsz, seq, d), F32),
        scratch_shapes=[
            pltpu.VMEM((ts + CARRY_A, d), F32),
            pltpu.VMEM((ts + CARRY_B, d), F32),
            pltpu.VMEM((ts, d), F32),
            pltpu.VMEM((ts, d), F32),
            pltpu.VMEM((ts, d), F32),
            pltpu.VMEM((ts, 2 * d), F32),
        ],
        compiler_params=pltpu.CompilerParams(
            dimension_semantics=("arbitrary", "arbitrary"), vmem_limit_bytes=VMEM_LIMIT),
        name="token_mixer",
    )(x, mod, w_in, b_in, caw, cab, lag, lab, cbw, w_pa, b_pa, w_pb, w_o, g, b)


def _router_kernel(x_ref, mod_ref, rwh_ref, rwl_ref, rb_ref, xs_ref, meta_ref, tab_ref):
    tt = x_ref.shape[0]
    d = D_MODEL

    sh2 = mod_ref[0, 0, 3:4, :]
    sc2 = mod_ref[0, 0, 4:5, :]
    u2 = _ln(x_ref[...]) * (1.0 + sc2) + sh2
    u2h = u2.astype(BF16)
    u2l = (u2 - u2h.astype(F32)).astype(BF16)
    logits = (jnp.dot(u2h, rwh_ref[...], preferred_element_type=F32)
              + jnp.dot(u2l, rwh_ref[...], preferred_element_type=F32)
              + jnp.dot(u2h, rwl_ref[...], preferred_element_type=F32)) + rb_ref[...]
    lane = lax.broadcasted_iota(jnp.int32, (tt, LANES), 1)
    lane_f = lane.astype(F32)
    neg_inf = jnp.float32(-jnp.inf)
    work = jnp.where(lane < N_EXPERTS, logits, neg_inf)
    sel = jnp.zeros((tt, LANES), F32)
    vals, idxs = [], []
    for _ in range(TOP_K):
        m = jnp.max(work, axis=-1, keepdims=True)
        idx = jnp.min(jnp.where(work == m, lane_f, float(LANES)), axis=-1, keepdims=True)
        hit = lane_f == idx
        vals.append(m)
        idxs.append(idx)
        work = jnp.where(hit, neg_inf, work)
        sel = sel + hit.astype(F32)
    exps = [jnp.exp(v - vals[0]) for v in vals]
    inv = 1.0 / (exps[0] + exps[1] + exps[2] + exps[3])

    n_e = jnp.sum(sel, axis=0, keepdims=True)
    c8 = jnp.floor((n_e + float(GRANULE - 1)) * (1.0 / GRANULE))
    er = lax.broadcasted_iota(jnp.int32, (LANES, LANES), 0)
    ec = lax.broadcasted_iota(jnp.int32, (LANES, LANES), 1)
    upper = (er < ec).astype(BF16)
    off8 = jnp.dot(jnp.broadcast_to(c8, (SUBLANES, LANES)).astype(BF16), upper,
                   preferred_element_type=F32)[0:1]
    row = lax.broadcasted_iota(jnp.int32, (tt, tt), 0)
    col = lax.broadcasted_iota(jnp.int32, (tt, tt), 1)
    tri = (col < row).astype(BF16)
    prefix = jnp.dot(tri, sel.astype(BF16), preferred_element_type=F32)
    slot_dense = off8 * float(GRANULE) + prefix

    meta = jnp.zeros((tt, LANES), F32)
    for k in range(TOP_K):
        pos_k = jnp.sum(jnp.where(lane_f == idxs[k], slot_dense, 0.0), axis=-1, keepdims=True)
        meta = jnp.where(lane == k, pos_k, meta)
        meta = jnp.where(lane == TOP_K + k, exps[k] * inv, meta)
    meta_ref[...] = meta

    sub = lax.broadcasted_iota(jnp.int32, (SUBLANES, LANES), 0)
    tab_ref[0] = jnp.where(sub == 0, c8, jnp.where(sub == 1, off8, 0.0))

    meta_t = meta.T
    slot_iota = lax.broadcasted_iota(jnp.int32, (TILE_ROWS, tt), 0).astype(F32)
    pt = jnp.zeros((TILE_ROWS, tt), F32)
    gm = jnp.zeros((TILE_ROWS, tt), F32)
    for k in range(TOP_K):
        hit = slot_iota == meta_t[k:k + 1, :]
        pt = pt + hit.astype(F32)
        gm = gm + jnp.where(hit, meta_t[TOP_K + k:TOP_K + k + 1, :], 0.0)
    xs_ref[:, 0:HALF_D] = _pack_halves(jnp.dot(pt.astype(BF16), u2h, preferred_element_type=F32))
    gate_row = jnp.sum(gm, axis=-1, keepdims=True)
    xs_ref[:, HALF_D:XS_WORDS] = lax.bitcast_convert_type(
        jnp.broadcast_to(gate_row, (TILE_ROWS, LANES)), jnp.uint32)


def _router(x1, mod, l, rw_hi, rw_lo, rb_pad, seq):
    n_tok, d = x1.shape
    tt = TOK_TILE
    n_tiles = n_tok // tt
    tiles_per_seq = seq // tt
    return pl.pallas_call(
        _router_kernel,
        grid=(n_tiles,),
        in_specs=[
            pl.BlockSpec((tt, d), lambda i: (i, 0)),
            pl.BlockSpec((1, 1, 6, d), lambda i: (l, i // tiles_per_seq, 0, 0)),
            pl.BlockSpec((None, d, LANES), lambda i: (l, 0, 0)),
            pl.BlockSpec((None, d, LANES), lambda i: (l, 0, 0)),
            pl.BlockSpec((None, 1, LANES), lambda i: (l, 0, 0)),
        ],
        out_specs=[
            pl.BlockSpec((TILE_ROWS, XS_WORDS), lambda i: (i, 0)),
            pl.BlockSpec((tt, LANES), lambda i: (i, 0)),
            pl.BlockSpec((1, SUBLANES, LANES), lambda i: (i, 0, 0)),
        ],
        out_shape=[
            jax.ShapeDtypeStruct((n_tiles * TILE_ROWS, XS_WORDS), jnp.uint32),
            jax.ShapeDtypeStruct((n_tok, LANES), F32),
            jax.ShapeDtypeStruct((n_tiles, SUBLANES, LANES), F32),
        ],
        compiler_params=pltpu.CompilerParams(
            dimension_semantics=("arbitrary",), vmem_limit_bytes=VMEM_LIMIT),
        name="router",
    )(x1, mod, rw_hi, rw_lo, rb_pad)


def _granule_plan(tab, n_tiles):
    c8 = tab[:, 0, :N_EXPERTS].astype(jnp.int32).T
    off8 = tab[:, 1, :N_EXPERTS].astype(jnp.int32).T
    used = jnp.sum(c8, axis=0, keepdims=True)
    c8 = jnp.concatenate([c8, TILE_ROWS // GRANULE - used], axis=0)
    off8 = jnp.concatenate([off8, used], axis=0)
    per_e = jnp.sum(c8, axis=1)
    padded = (per_e + BLOCK_GRANULES - 1) // BLOCK_GRANULES * BLOCK_GRANULES
    e_end = jnp.cumsum(padded)
    e_start = e_end - padded
    strip_start = (e_start[:, None] + jnp.cumsum(c8, axis=1) - c8).reshape(-1)
    cnt_flat = c8.reshape(-1)
    off_flat = off8.reshape(-1)
    slot = jnp.arange(N_BLOCKS * BLOCK_GRANULES, dtype=jnp.int32)
    before = strip_start[None, :] <= slot[:, None]
    tile_flat = jnp.arange(c8.size, dtype=jnp.int32) % n_tiles

    def at_strip(table):
        diff = table - jnp.concatenate([jnp.zeros((1,), jnp.int32), table[:-1]])
        return jnp.sum(jnp.where(before, diff[None, :], 0), axis=1)

    g = slot - at_strip(strip_start)
    valid = g < at_strip(cnt_flat)
    rows = at_strip(tile_flat * TILE_ROWS + off_flat * GRANULE) + g * GRANULE
    src = jnp.where(valid, rows, 0)
    trash = n_tiles * TILE_ROWS + (slot % BLOCK_GRANULES) * GRANULE
    dst = jnp.where(valid, rows, trash)
    block_start = jnp.arange(N_BLOCKS, dtype=jnp.int32) * BLOCK_GRANULES
    block_e = jnp.minimum(
        jnp.sum((e_end[None, :] <= block_start[:, None]).astype(jnp.int32), axis=1), N_EXPERTS - 1)
    n_used = e_end[-1:] // BLOCK_GRANULES
    n_real = e_end[N_EXPERTS - 1:N_EXPERTS] // BLOCK_GRANULES
    end_blk = e_end[:N_EXPERTS] // BLOCK_GRANULES
    expert_ids = jnp.arange(N_EXPERTS, dtype=jnp.int32)
    run_end = jnp.sum(jnp.where(block_e[:, None] == expert_ids[None, :], end_blk[None, :], 0), axis=1)
    after = jnp.minimum(jnp.sum((end_blk[None, :] <= run_end[:, None]).astype(jnp.int32), axis=1), N_EXPERTS - 1)
    next_e = jnp.where(run_end < n_real, after, -1)
    i32 = lambda a: a.astype(jnp.int32)
    return i32(block_e), i32(next_e), i32(n_used), i32(n_real), i32(src), i32(dst)


def _granule_copy(hbm_ref, row, buf_ref, slot, i, sem, to_hbm):
    hbm = hbm_ref.at[pl.ds(pl.multiple_of(row, GRANULE), GRANULE)]
    vmem = buf_ref.at[slot, pl.ds(i * GRANULE, GRANULE)]
    return pltpu.make_async_copy(vmem, hbm, sem) if to_hbm else pltpu.make_async_copy(hbm, vmem, sem)


def _block_copy(hbm_ref, buf_ref, slot, sem, to_hbm):
    hbm = hbm_ref.at[pl.ds(0, EXPERT_BLOCK)]
    vmem = buf_ref.at[slot]
    return pltpu.make_async_copy(vmem, hbm, sem) if to_hbm else pltpu.make_async_copy(hbm, vmem, sem)


def _expert_kernel(layer, be_ref, ne_ref, nb_ref, nr_ref, src_ref, dst_ref, xs_ref, w1_ref, b1_ref, w2_ref,
                   b2_ref, ys_ref, w1s, w2s, w1b, w2b, xbuf, ybuf, in_sem, out_sem, w_sem):
    b = pl.program_id(0)
    n_used = nb_ref[0]
    n_real = nr_ref[0]
    e = be_ref[b]
    prev = be_ref[jnp.maximum(b - 1, 0)]
    slot = b % 2

    def start_gather(blk, s):
        for i in range(BLOCK_GRANULES):
            _granule_copy(xs_ref, src_ref[blk * BLOCK_GRANULES + i], xbuf, s, i, in_sem.at[s], False).start()

    def weight_copies(expert):
        return (pltpu.make_async_copy(w1_ref.at[layer, expert], w1s, w_sem.at[0]),
                pltpu.make_async_copy(w2_ref.at[layer, expert], w2s, w_sem.at[1]))

    @pl.when(b == 0)
    def _():
        for cp in weight_copies(e):
            cp.start()

    @pl.when(jnp.logical_and(b < n_real, jnp.logical_or(b == 0, e != prev)))
    def _():
        for cp in weight_copies(e):
            cp.wait()
        w1b[...] = w1s[...].astype(BF16)
        w2b[...] = w2s[...].astype(BF16)

        @pl.when(ne_ref[b] >= 0)
        def _():
            for cp in weight_copies(ne_ref[b]):
                cp.start()

    @pl.when(b == 0)
    def _():
        start_gather(0, 0)
        ybuf[1] = jnp.zeros(ybuf.shape[1:], jnp.uint32)
        pad_rows = ys_ref.at[pl.ds(ys_ref.shape[0] - EXPERT_BLOCK, EXPERT_BLOCK)]
        fill = pltpu.make_async_copy(ybuf.at[1], pad_rows, out_sem.at[1])
        fill.start()
        fill.wait()

    @pl.when(b + 1 < n_used)
    def _():
        start_gather(b + 1, 1 - slot)

    @pl.when(b < n_used)
    def _():
        _block_copy(xs_ref, xbuf, slot, in_sem.at[slot], False).wait()

        @pl.when(b >= 2)
        def _():
            _block_copy(ys_ref, ybuf, slot, out_sem.at[slot], True).wait()

        @pl.when(b < n_real)
        def _():
            xw = xbuf[slot]
            gate = lax.bitcast_convert_type(xw[:, HALF_D:HALF_D + 1], F32)
            h = jnp.dot(_unpack_halves(xw[:, 0:HALF_D]), w1b[...], preferred_element_type=F32) + b1_ref[0, 0]
            h_glu = jnp.minimum(h[:, 0:D_FF], SWIGLU_LIMIT)
            h_lin = jnp.clip(h[:, D_FF:2 * D_FF], -SWIGLU_LIMIT, SWIGLU_LIMIT)
            a = h_glu * _sigmoid(h_glu, SWIGLU_ALPHA) * (h_lin + 1.0)
            y = (_bdot(a, w2b[...]) + b2_ref[0, 0]) * gate
            ybuf[slot] = _pack_halves(y.astype(BF16).astype(F32))

        @pl.when(b >= n_real)
        def _():
            ybuf[slot] = jnp.zeros(ybuf.shape[1:], jnp.uint32)

        for i in range(BLOCK_GRANULES):
            _granule_copy(ys_ref, dst_ref[b * BLOCK_GRANULES + i], ybuf, slot, i, out_sem.at[slot], True).start()

    @pl.when(b == n_used - 1)
    def _():
        @pl.when(b >= 1)
        def _():
            _block_copy(ys_ref, ybuf, 1 - slot, out_sem.at[1 - slot], True).wait()

        _block_copy(ys_ref, ybuf, slot, out_sem.at[slot], True).wait()


def _experts(block_e, next_e, n_used, n_real, src, dst, xs, l, w1, b1, w2, b2, n_tiles):
    d = D_MODEL
    f2 = w1.shape[-1]
    ys_rows = n_tiles * TILE_ROWS + EXPERT_BLOCK
    expert_block = lambda i, be, ne, nb, nr, s, t: (l, be[i], 0, 0)
    grid_spec = pltpu.PrefetchScalarGridSpec(
        num_scalar_prefetch=6,
        grid=(N_BLOCKS,),
        in_specs=[
            pl.BlockSpec(memory_space=pl.ANY),
            pl.BlockSpec(memory_space=pl.ANY),
            pl.BlockSpec((1, 1, 1, f2), expert_block),
            pl.BlockSpec(memory_space=pl.ANY),
            pl.BlockSpec((1, 1, 1, d), expert_block),
        ],
        out_specs=pl.BlockSpec(memory_space=pl.ANY),
        scratch_shapes=[
            pltpu.VMEM((d, f2), F32),
            pltpu.VMEM((D_FF, d), F32),
            pltpu.VMEM((d, f2), BF16),
            pltpu.VMEM((D_FF, d), BF16),
            pltpu.VMEM((2, EXPERT_BLOCK, XS_WORDS), jnp.uint32),
            pltpu.VMEM((2, EXPERT_BLOCK, HALF_D), jnp.uint32),
            pltpu.SemaphoreType.DMA((2,)),
            pltpu.SemaphoreType.DMA((2,)),
            pltpu.SemaphoreType.DMA((2,)),
        ],
    )
    return pl.pallas_call(
        functools.partial(_expert_kernel, l),
        grid_spec=grid_spec,
        out_shape=jax.ShapeDtypeStruct((ys_rows, HALF_D), jnp.uint32),
        compiler_params=pltpu.CompilerParams(
            dimension_semantics=("arbitrary",), vmem_limit_bytes=VMEM_LIMIT),
        name="experts",
    )(block_e, next_e, n_used, n_real, src, dst, xs, w1, b1, w2, b2)


def _combine_kernel(ys_ref, meta_ref, x_ref, mod_ref, g_ref, b_ref, o_ref):
    tt = x_ref.shape[0]
    slot_iota = lax.broadcasted_iota(jnp.int32, (tt, TILE_ROWS), 1).astype(F32)
    p = jnp.zeros((tt, TILE_ROWS), F32)
    for k in range(TOP_K):
        p = p + (slot_iota == meta_ref[:, k:k + 1]).astype(F32)
    y = jnp.dot(p.astype(BF16), _unpack_halves(ys_ref[...]), preferred_element_type=F32)
    g2 = mod_ref[0, 0, 5:6, :]
    o_ref[...] = _ln(DEEPNORM_ALPHA * x_ref[...] + g2 * y) * g_ref[...] + b_ref[...]


def _combine(ys, meta, x1, mod, l, g, b, seq):
    n_tok, d = x1.shape
    tt = TOK_TILE
    tiles_per_seq = seq // tt
    return pl.pallas_call(
        _combine_kernel,
        grid=(n_tok // tt,),
        in_specs=[
            pl.BlockSpec((TILE_ROWS, HALF_D), lambda i: (i, 0)),
            pl.BlockSpec((tt, LANES), lambda i: (i, 0)),
            pl.BlockSpec((tt, d), lambda i: (i, 0)),
            pl.BlockSpec((1, 1, 6, d), lambda i: (l, i // tiles_per_seq, 0, 0)),
            pl.BlockSpec((1, d), lambda i: (0, 0)),
            pl.BlockSpec((1, d), lambda i: (0, 0)),
        ],
        out_specs=pl.BlockSpec((tt, d), lambda i: (i, 0)),
        out_shape=jax.ShapeDtypeStruct((n_tok, d), F32),
        compiler_params=pltpu.CompilerParams(
            dimension_semantics=("arbitrary",), vmem_limit_bytes=VMEM_LIMIT),
        name="combine",
    )(ys, meta, x1, mod, g, b)


def kernel(x, c, ada_w, ada_b, w_in, b_in, conv_a_w, conv_a_b, ln_a_g, ln_a_b, conv_b_w, w_pa, b_pa,
           w_pb, w_o, ln1_g, ln1_b, router_w, router_b, w1, b1, w2, b2, ln2_g, ln2_b):
    bsz, seq, d = x.shape
    n_layers = ada_w.shape[0]
    n_tok = bsz * seq
    n_tiles = n_tok // TOK_TILE
    assert (n_tok, d) == (N_TOKENS, D_MODEL)

    mod = _ada_mod(c, ada_w, ada_b)

    w_in_b = w_in.astype(BF16)
    w_pa_b = w_pa.astype(BF16)
    w_pb_b = w_pb.astype(BF16)
    w_o_b = w_o.astype(BF16)
    caw = jnp.repeat(conv_a_w, SUBLANES, axis=1)
    cbw = jnp.repeat(conv_b_w, SUBLANES, axis=1)
    rw_pad = jnp.pad(router_w, ((0, 0), (0, 0), (0, LANES - N_EXPERTS)))
    rw_hi = rw_pad.astype(BF16)
    rw_lo = (rw_pad - rw_hi.astype(F32)).astype(BF16)
    rb_pad = jnp.pad(router_b, ((0, 0), (0, LANES - N_EXPERTS)))[:, None, :]
    b1r = b1.reshape(n_layers, N_EXPERTS, 1, 2 * D_FF)
    b2r = b2.reshape(n_layers, N_EXPERTS, 1, d)
    row = lambda a, l: a[l][None, :]
    rows = lambda a: a[:, None, :]

    for l in range(n_layers):
        x = _token_mixer(x, mod, l, w_in_b, rows(b_in), caw, rows(conv_a_b), rows(ln_a_g), rows(ln_a_b),
                         cbw, w_pa_b, rows(b_pa), w_pb_b, w_o_b, rows(ln1_g), rows(ln1_b))
        x1 = x.reshape(n_tok, d)
        xs, meta, tab = _router(x1, mod, l, rw_hi, rw_lo, rb_pad, seq)
        block_e, next_e, n_used, n_real, src, dst = _granule_plan(tab, n_tiles)
        ys = _experts(block_e, next_e, n_used, n_real, src, dst, xs, l, w1, b1r, w2, b2r, n_tiles)
        x = _combine(ys, meta, x1, mod, l, row(ln2_g, l), row(ln2_b, l), seq)
        x = x.reshape(bsz, seq, d)
    return x
```

```python
import functools

import jax
import jax.numpy as jnp
from jax import lax
from jax.experimental import pallas as pl
from jax.experimental.pallas import tpu as pltpu

D_MODEL = 1024
DEPTH = 4
CONV_A = 31
CONV_B = 3
N_EXPERTS = 32
TOP_K = 4
D_FF = D_MODEL
SWIGLU_LIMIT = 7.0
SWIGLU_ALPHA = 1.702
EXPERT_BLOCK = 256
DEEPNORM_ALPHA = (2.0 * DEPTH) ** 0.25
LN_EPS = 1e-5

LANES = 128
SUBLANES = 8
VMEM_LIMIT = 56 * 1024 * 1024

MIX_TS = 512
CONV_RC = 64
CONV_LC = 256
PROJ_NC = 512
CARRY_A = 32
CARRY_B = 8
TOK_TILE = 256
GRANULE = SUBLANES
BLOCK_GRANULES = EXPERT_BLOCK // GRANULE
N_TOKENS = 16384
TILE_ROWS = 1280
HALF_D = D_MODEL // 2
XS_WORDS = HALF_D + LANES
assert TILE_ROWS >= TOP_K * TOK_TILE + N_EXPERTS * (GRANULE - 1) and TILE_ROWS % LANES == 0
N_BLOCKS = -(-((N_TOKENS // TOK_TILE) * (TILE_ROWS // GRANULE)
               + (N_EXPERTS + 1) * (BLOCK_GRANULES - 1)) // BLOCK_GRANULES)

F32 = jnp.float32
BF16 = jnp.bfloat16


def _ln(x):
    mu = jnp.mean(x, axis=-1, keepdims=True)
    xc = x - mu
    var = jnp.mean(xc * xc, axis=-1, keepdims=True)
    return xc * lax.rsqrt(var + LN_EPS)


NEG_LOG2E = -1.4426950408889634


def _sigmoid(x, scale=1.0):
    return 1.0 / (1.0 + jnp.exp2(x * (scale * NEG_LOG2E)))


def _bdot(a, b):
    return jnp.dot(a.astype(BF16), b, preferred_element_type=F32)


HIGH_HALF = 0xFFFF0000


def _pack_halves(x):
    h = x.shape[1] // 2
    lo = lax.shift_right_logical(lax.bitcast_convert_type(x[:, :h], jnp.uint32), jnp.uint32(16))
    hi = lax.bitcast_convert_type(x[:, h:], jnp.uint32) & jnp.uint32(HIGH_HALF)
    return hi | lo


def _unpack_halves(w):
    lo = lax.bitcast_convert_type(lax.shift_left(w, jnp.uint32(16)), F32)
    hi = lax.bitcast_convert_type(w & jnp.uint32(HIGH_HALF), F32)
    return jnp.concatenate([lo.astype(BF16), hi.astype(BF16)], axis=1)


def _ada_kernel(c_ref, w_ref, b_ref, o_ref):
    c = c_ref[...]
    cond = c * _sigmoid(c)
    o_ref[0] = jnp.dot(cond, w_ref[0], preferred_element_type=F32,
                       precision=lax.Precision.HIGHEST) + b_ref[0]


def _ada_mod(c, ada_w, ada_b):
    bsz, d = c.shape
    n_layers = ada_w.shape[0]
    c_pad = jnp.zeros((SUBLANES, d), F32).at[:bsz].set(c)
    out = pl.pallas_call(
        _ada_kernel,
        grid=(n_layers, 6),
        in_specs=[
            pl.BlockSpec((SUBLANES, d), lambda l, j: (0, 0)),
            pl.BlockSpec((1, d, d), lambda l, j: (l, 0, j)),
            pl.BlockSpec((1, 1, d), lambda l, j: (l, 0, j)),
        ],
        out_specs=pl.BlockSpec((1, SUBLANES, d), lambda l, j: (l, 0, j)),
        out_shape=jax.ShapeDtypeStruct((n_layers, SUBLANES, 6 * d), F32),
        compiler_params=pltpu.CompilerParams(
            dimension_semantics=("arbitrary", "arbitrary"), vmem_limit_bytes=VMEM_LIMIT),
        name="ada_mod",
    )(c_pad, ada_w, ada_b.reshape(n_layers, 1, 6 * d))
    return out[:, :bsz].reshape(n_layers, bsz, 6, d)


def _conv_chunk(ext_ref, w_ref, out_ref, n_taps, carry_rows, r0, c0, dep=None):
    base = carry_rows - n_taps + 1
    out = None
    for b in range(SUBLANES):
        win = CONV_RC if b == 0 else CONV_RC + SUBLANES
        z = None
        for k in range(n_taps):
            if (base + k) % SUBLANES != b:
                continue
            a8 = base + k - b
            w8 = w_ref[pl.ds(SUBLANES * k, SUBLANES), c0:c0 + CONV_LC]
            term = jnp.tile(w8, (win // SUBLANES, 1)) * ext_ref[pl.ds(r0 + a8, win), c0:c0 + CONV_LC]
            z = term if z is None else z + term
        if z is None:
            continue
        if b != 0:
            z = pltpu.roll(z, win - b, axis=0)[0:CONV_RC]
        out = z if out is None else out + z
    if dep is not None:
        out = out + jnp.tile(dep, (CONV_RC // SUBLANES, CONV_LC // LANES))
    out_ref[pl.ds(r0, CONV_RC), c0:c0 + CONV_LC] = out


def _mixer_kernel(x_ref, mod_ref, w_in_ref, b_in_ref, caw_ref, cab_ref, lag_ref, lab_ref,
                  cbw_ref, w_pa_ref, b_pa_ref, w_pb_ref, w_o_ref, g_ref, b_ref,
                  o_ref, ext_a, ext_b, cv_a, cv_b, gb_s, sg_s):
    ts = x_ref.shape[1]
    d = D_MODEL

    @pl.when(pl.program_id(1) == 0)
    def _():
        ext_a[pl.ds(0, CARRY_A), :] = jnp.zeros((CARRY_A, d), F32)
        ext_b[pl.ds(0, CARRY_B), :] = jnp.zeros((CARRY_B, d), F32)

    xt = x_ref[0]
    sh1 = mod_ref[0, 0, 0:1, :]
    sc1 = mod_ref[0, 0, 1:2, :]
    g1 = mod_ref[0, 0, 2:3, :]
    u = (_ln(xt) * (1.0 + sc1) + sh1).astype(BF16)

    def proj(c0, n):
        return jnp.dot(u, w_in_ref[:, c0:c0 + n], preferred_element_type=F32) + b_in_ref[:, c0:c0 + n]

    cur_a = pl.ds(CARRY_A, ts)
    cur_b = pl.ds(CARRY_B, ts)
    nc = PROJ_NC

    def tie(z):
        bits = lax.bitcast_convert_type(z[0:SUBLANES, 0:LANES], jnp.uint32)
        zero = lax.shift_right_logical(lax.shift_right_logical(bits, jnp.uint32(16)), jnp.uint32(16))
        return lax.bitcast_convert_type(zero, F32)

    za = proj(0, 2 * d)
    ext_a[cur_a, :] = za[:, 0:d] * _sigmoid(za[:, d:2 * d])

    def b_gate(c):
        z = proj(2 * d + c, nc)
        gb_s[:, c:c + nc] = z
        return tie(z)

    def b_c(c):
        z = proj(3 * d + c, nc)
        ext_b[cur_b, c:c + nc] = z
        return tie(z)

    def b_h(c):
        z = proj(4 * d + c, nc)
        ext_b[cur_b, c:c + nc] = ext_b[cur_b, c:c + nc] * z
        return tie(z)

    def merge_gate(c):
        z = proj(5 * d + c, nc)
        sg_s[:, c:c + nc] = _sigmoid(z)
        return tie(z)

    jobs = [lambda c=c, f=f: f(c) for f in (b_gate, b_c, b_h) for c in range(0, d, nc)]
    jobs += [lambda c=c: merge_gate(c) for c in range(0, 2 * d, nc)]
    conv_chunks = [(r0, c0) for r0 in range(0, ts, CONV_RC) for c0 in range(0, d, CONV_LC)]
    assert len(jobs) < len(conv_chunks)
    stride = (len(conv_chunks) - 1) // len(jobs)
    dep = None
    for i, (r0, c0) in enumerate(conv_chunks):
        nxt = jobs[i // stride]() if i % stride == 0 and i // stride < len(jobs) else None
        _conv_chunk(ext_a, caw_ref, cv_a, CONV_A, CARRY_A, r0, c0, dep)
        dep = nxt
    for r0, c0 in conv_chunks:
        _conv_chunk(ext_b, cbw_ref, cv_b, CONV_B, CARRY_B, r0, c0)

    ext_a[pl.ds(0, CARRY_A), :] = ext_a[pl.ds(ts, CARRY_A), :]
    ext_b[pl.ds(0, CARRY_B), :] = ext_b[pl.ds(ts, CARRY_B), :]

    ya = _ln(cv_a[...] + cab_ref[...]) * lag_ref[...] + lab_ref[...]
    ya = ya * _sigmoid(ya)
    ya = _bdot(ya, w_pa_ref[...]) + b_pa_ref[...]
    yb = _bdot(gb_s[...] * cv_b[...], w_pb_ref[...])

    m = sg_s[:, 0:d] * ya + sg_s[:, d:2 * d] * yb
    y = _bdot(m, w_o_ref[...])
    o_ref[0] = _ln(DEEPNORM_ALPHA * xt + g1 * y) * g_ref[...] + b_ref[...]


def _layer_spec(l, shape):
    nd = len(shape)
    return pl.BlockSpec((None,) + shape, lambda b, s: (l,) + (0,) * nd, pipeline_mode=pl.Buffered(1))


def _token_mixer(x, mod, l, w_in, b_in, caw, cab, lag, lab, cbw, w_pa, b_pa, w_pb, w_o, g, b):
    bsz, seq, d = x.shape
    ts = MIX_TS
    n_in = w_in.shape[-1]
    return pl.pallas_call(
        _mixer_kernel,
        grid=(bsz, seq // ts),
        in_specs=[
            pl.BlockSpec((1, ts, d), lambda b, s: (b, s, 0)),
            pl.BlockSpec((1, 1, 6, d), lambda b, s: (l, b, 0, 0)),
            _layer_spec(l, (d, n_in)),
            _layer_spec(l, (1, n_in)),
            _layer_spec(l, (SUBLANES * CONV_A, d)),
            _layer_spec(l, (1, d)),
            _layer_spec(l, (1, d)),
            _layer_spec(l, (1, d)),
            _layer_spec(l, (SUBLANES * CONV_B, d)),
            _layer_spec(l, (d, d)),
            _layer_spec(l, (1, d)),
            _layer_spec(l, (d, d)),
            _layer_spec(l, (d, d)),
            _layer_spec(l, (1, d)),
            _layer_spec(l, (1, d)),
        ],
        out_specs=pl.BlockSpec((1, ts, d), lambda b, s: (b, s, 0)),
        out_shape=jax.ShapeDtypeStruct((bsz, seq, d), F32),
        scratch_shapes=[
            pltpu.VMEM((ts + CARRY_A, d), F32),
            pltpu.VMEM((ts + CARRY_B, d), F32),
            pltpu.VMEM((ts, d), F32),
            pltpu.VMEM((ts, d), F32),
            pltpu.VMEM((ts, d), F32),
            pltpu.VMEM((ts, 2 * d), F32),
        ],
        compiler_params=pltpu.CompilerParams(
            dimension_semantics=("arbitrary", "arbitrary"), vmem_limit_bytes=VMEM_LIMIT),
        name="token_mixer",
    )(x, mod, w_in, b_in, caw, cab, lag, lab, cbw, w_pa, b_pa, w_pb, w_o, g, b)


def _router_kernel(x_ref, mod_ref, rwh_ref, rwl_ref, rb_ref, xs_ref, meta_ref, tab_ref):
    tt = x_ref.shape[0]
    d = D_MODEL

    sh2 = mod_ref[0, 0, 3:4, :]
    sc2 = mod_ref[0, 0, 4:5, :]
    u2 = _ln(x_ref[...]) * (1.0 + sc2) + sh2
    u2h = u2.astype(BF16)
    u2l = (u2 - u2h.astype(F32)).astype(BF16)
    logits = (jnp.dot(u2h, rwh_ref[...], preferred_element_type=F32)
              + jnp.dot(u2l, rwh_ref[...], preferred_element_type=F32)
              + jnp.dot(u2h, rwl_ref[...], preferred_element_type=F32)) + rb_ref[...]
    lane = lax.broadcasted_iota(jnp.int32, (tt, LANES), 1)
    lane_f = lane.astype(F32)
    neg_inf = jnp.float32(-jnp.inf)
    work = jnp.where(lane < N_EXPERTS, logits, neg_inf)
    sel = jnp.zeros((tt, LANES), F32)
    vals, idxs = [], []
    for _ in range(TOP_K):
        m = jnp.max(work, axis=-1, keepdims=True)
        idx = jnp.min(jnp.where(work == m, lane_f, float(LANES)), axis=-1, keepdims=True)
        hit = lane_f == idx
        vals.append(m)
        idxs.append(idx)
        work = jnp.where(hit, neg_inf, work)
        sel = sel + hit.astype(F32)
    exps = [jnp.exp(v - vals[0]) for v in vals]
    inv = 1.0 / (exps[0] + exps[1] + exps[2] + exps[3])

    n_e = jnp.sum(sel, axis=0, keepdims=True)
    c8 = jnp.floor((n_e + float(GRANULE - 1)) * (1.0 / GRANULE))
    er = lax.broadcasted_iota(jnp.int32, (LANES, LANES), 0)
    ec = lax.broadcasted_iota(jnp.int32, (LANES, LANES), 1)
    upper = (er < ec).astype(BF16)
    off8 = jnp.dot(jnp.broadcast_to(c8, (SUBLANES, LANES)).astype(BF16), upper,
                   preferred_element_type=F32)[0:1]
    row = lax.broadcasted_iota(jnp.int32, (tt, tt), 0)
    col = lax.broadcasted_iota(jnp.int32, (tt, tt), 1)
    tri = (col < row).astype(BF16)
    prefix = jnp.dot(tri, sel.astype(BF16), preferred_element_type=F32)
    slot_dense = off8 * float(GRANULE) + prefix

    meta = jnp.zeros((tt, LANES), F32)
    for k in range(TOP_K):
        pos_k = jnp.sum(jnp.where(lane_f == idxs[k], slot_dense, 0.0), axis=-1, keepdims=True)
        meta = jnp.where(lane == k, pos_k, meta)
        meta = jnp.where(lane == TOP_K + k, exps[k] * inv, meta)
    meta_ref[...] = meta

    sub = lax.broadcasted_iota(jnp.int32, (SUBLANES, LANES), 0)
    tab_ref[0] = jnp.where(sub == 0, c8, jnp.where(sub == 1, off8, 0.0))

    meta_t = meta.T
    slot_iota = lax.broadcasted_iota(jnp.int32, (TILE_ROWS, tt), 0).astype(F32)
    pt = jnp.zeros((TILE_ROWS, tt), F32)
    gm = jnp.zeros((TILE_ROWS, tt), F32)
    for k in range(TOP_K):
        hit = slot_iota == meta_t[k:k + 1, :]
        pt = jnp.where(hit, 1.0, pt)
        gm = jnp.where(hit, meta_t[TOP_K + k:TOP_K + k + 1, :], gm)
    xs_ref[:, 0:HALF_D] = _pack_halves(jnp.dot(pt.astype(BF16), u2h, preferred_element_type=F32))
    gate_row = jnp.sum(gm, axis=-1, keepdims=True)
    xs_ref[:, HALF_D:XS_WORDS] = lax.bitcast_convert_type(
        jnp.broadcast_to(gate_row, (TILE_ROWS, LANES)), jnp.uint32)


def _router(x1, mod, l, rw_hi, rw_lo, rb_pad, seq):
    n_tok, d = x1.shape
    tt = TOK_TILE
    n_tiles = n_tok // tt
    tiles_per_seq = seq // tt
    return pl.pallas_call(
        _router_kernel,
        grid=(n_tiles,),
        in_specs=[
            pl.BlockSpec((tt, d), lambda i: (i, 0)),
            pl.BlockSpec((1, 1, 6, d), lambda i: (l, i // tiles_per_seq, 0, 0)),
            pl.BlockSpec((None, d, LANES), lambda i: (l, 0, 0)),
            pl.BlockSpec((None, d, LANES), lambda i: (l, 0, 0)),
            pl.BlockSpec((None, 1, LANES), lambda i: (l, 0, 0)),
        ],
        out_specs=[
            pl.BlockSpec((TILE_ROWS, XS_WORDS), lambda i: (i, 0)),
            pl.BlockSpec((tt, LANES), lambda i: (i, 0)),
            pl.BlockSpec((1, SUBLANES, LANES), lambda i: (i, 0, 0)),
        ],
        out_shape=[
            jax.ShapeDtypeStruct((n_tiles * TILE_ROWS, XS_WORDS), jnp.uint32),
            jax.ShapeDtypeStruct((n_tok, LANES), F32),
            jax.ShapeDtypeStruct((n_tiles, SUBLANES, LANES), F32),
        ],
        compiler_params=pltpu.CompilerParams(
            dimension_semantics=("arbitrary",), vmem_limit_bytes=VMEM_LIMIT),
        name="router",
    )(x1, mod, rw_hi, rw_lo, rb_pad)


def _granule_plan(tab, n_tiles):
    c8 = tab[:, 0, :N_EXPERTS].astype(jnp.int32).T
    off8 = tab[:, 1, :N_EXPERTS].astype(jnp.int32).T
    used = jnp.sum(c8, axis=0, keepdims=True)
    c8 = jnp.concatenate([c8, TILE_ROWS // GRANULE - used], axis=0)
    off8 = jnp.concatenate([off8, used], axis=0)
    per_e = jnp.sum(c8, axis=1)
    padded = (per_e + BLOCK_GRANULES - 1) // BLOCK_GRANULES * BLOCK_GRANULES
    e_end = jnp.cumsum(padded)
    e_start = e_end - padded
    strip_start = (e_start[:, None] + jnp.cumsum(c8, axis=1) - c8).reshape(-1)
    cnt_flat = c8.reshape(-1)
    off_flat = off8.reshape(-1)
    slot = jnp.arange(N_BLOCKS * BLOCK_GRANULES, dtype=jnp.int32)
    before = strip_start[None, :] <= slot[:, None]
    tile_flat = jnp.arange(c8.size, dtype=jnp.int32) % n_tiles

    def at_strip(table):
        diff = table - jnp.concatenate([jnp.zeros((1,), jnp.int32), table[:-1]])
        return jnp.sum(jnp.where(before, diff[None, :], 0), axis=1)

    g = slot - at_strip(strip_start)
    valid = g < at_strip(cnt_flat)
    rows = at_strip(tile_flat * TILE_ROWS + off_flat * GRANULE) + g * GRANULE
    src = jnp.where(valid, rows, 0)
    trash = n_tiles * TILE_ROWS + (slot % BLOCK_GRANULES) * GRANULE
    dst = jnp.where(valid, rows, trash)
    block_start = jnp.arange(N_BLOCKS, dtype=jnp.int32) * BLOCK_GRANULES
    block_e = jnp.minimum(
        jnp.sum((e_end[None, :] <= block_start[:, None]).astype(jnp.int32), axis=1), N_EXPERTS - 1)
    n_used = e_end[-1:] // BLOCK_GRANULES
    n_real = e_end[N_EXPERTS - 1:N_EXPERTS] // BLOCK_GRANULES
    end_blk = e_end[:N_EXPERTS] // BLOCK_GRANULES
    expert_ids = jnp.arange(N_EXPERTS, dtype=jnp.int32)
    run_end = jnp.sum(jnp.where(block_e[:, None] == expert_ids[None, :], end_blk[None, :], 0), axis=1)
    after = jnp.minimum(jnp.sum((end_blk[None, :] <= run_end[:, None]).astype(jnp.int32), axis=1), N_EXPERTS - 1)
    next_e = jnp.where(run_end < n_real, after, -1)
    i32 = lambda a: a.astype(jnp.int32)
    return i32(block_e), i32(next_e), i32(n_used), i32(n_real), i32(src), i32(dst)


def _granule_copy(hbm_ref, row, buf_ref, slot, i, sem, to_hbm):
    hbm = hbm_ref.at[pl.ds(pl.multiple_of(row, GRANULE), GRANULE)]
    vmem = buf_ref.at[slot, pl.ds(i * GRANULE, GRANULE)]
    return pltpu.make_async_copy(vmem, hbm, sem) if to_hbm else pltpu.make_async_copy(hbm, vmem, sem)


def _block_copy(hbm_ref, buf_ref, slot, sem, to_hbm):
    hbm = hbm_ref.at[pl.ds(0, EXPERT_BLOCK)]
    vmem = buf_ref.at[slot]
    return pltpu.make_async_copy(vmem, hbm, sem) if to_hbm else pltpu.make_async_copy(hbm, vmem, sem)


def _expert_kernel(layer, be_ref, ne_ref, nb_ref, nr_ref, src_ref, dst_ref, xs_ref, w1_ref, b1_ref, w2_ref,
                   b2_ref, ys_ref, w1s, w2s, w1b, w2b, xbuf, ybuf, in_sem, out_sem, w_sem):
    b = pl.program_id(0)
    n_used = nb_ref[0]
    n_real = nr_ref[0]
    e = be_ref[b]
    prev = be_ref[jnp.maximum(b - 1, 0)]
    slot = b % 2

    def start_gather(blk, s):
        for i in range(BLOCK_GRANULES):
            _granule_copy(xs_ref, src_ref[blk * BLOCK_GRANULES + i], xbuf, s, i, in_sem.at[s], False).start()

    def weight_copies(expert):
        return (pltpu.make_async_copy(w1_ref.at[layer, expert], w1s, w_sem.at[0]),
                pltpu.make_async_copy(w2_ref.at[layer, expert], w2s, w_sem.at[1]))

    @pl.when(b == 0)
    def _():
        for cp in weight_copies(e):
            cp.start()

    @pl.when(jnp.logical_and(b < n_real, jnp.logical_or(b == 0, e != prev)))
    def _():
        for cp in weight_copies(e):
            cp.wait()
        w1b[...] = w1s[...].astype(BF16)
        w2b[...] = w2s[...].astype(BF16)

        @pl.when(ne_ref[b] >= 0)
        def _():
            for cp in weight_copies(ne_ref[b]):
                cp.start()

    @pl.when(b == 0)
    def _():
        start_gather(0, 0)
        ybuf[1] = jnp.zeros(ybuf.shape[1:], jnp.uint32)
        pad_rows = ys_ref.at[pl.ds(ys_ref.shape[0] - EXPERT_BLOCK, EXPERT_BLOCK)]
        fill = pltpu.make_async_copy(ybuf.at[1], pad_rows, out_sem.at[1])
        fill.start()
        fill.wait()

    @pl.when(b + 1 < n_used)
    def _():
        start_gather(b + 1, 1 - slot)

    @pl.when(b < n_used)
    def _():
        _block_copy(xs_ref, xbuf, slot, in_sem.at[slot], False).wait()

        @pl.when(b >= 2)
        def _():
            _block_copy(ys_ref, ybuf, slot, out_sem.at[slot], True).wait()

        @pl.when(b < n_real)
        def _():
            xw = xbuf[slot]
            gate = lax.bitcast_convert_type(xw[:, HALF_D:HALF_D + 1], F32)
            h = jnp.dot(_unpack_halves(xw[:, 0:HALF_D]), w1b[...], preferred_element_type=F32) + b1_ref[0, 0]
            h_glu = jnp.minimum(h[:, 0:D_FF], SWIGLU_LIMIT)
            h_lin = jnp.clip(h[:, D_FF:2 * D_FF], -SWIGLU_LIMIT, SWIGLU_LIMIT)
            a = h_glu * _sigmoid(h_glu, SWIGLU_ALPHA) * (h_lin + 1.0)
            y = (_bdot(a, w2b[...]) + b2_ref[0, 0]) * gate
            ybuf[slot] = _pack_halves(y.astype(BF16).astype(F32))

        @pl.when(b >= n_real)
        def _():
            ybuf[slot] = jnp.zeros(ybuf.shape[1:], jnp.uint32)

        for i in range(BLOCK_GRANULES):
            _granule_copy(ys_ref, dst_ref[b * BLOCK_GRANULES + i], ybuf, slot, i, out_sem.at[slot], True).start()

    @pl.when(b == n_used - 1)
    def _():
        @pl.when(b >= 1)
        def _():
            _block_copy(ys_ref, ybuf, 1 - slot, out_sem.at[1 - slot], True).wait()

        _block_copy(ys_ref, ybuf, slot, out_sem.at[slot], True).wait()


def _experts(block_e, next_e, n_used, n_real, src, dst, xs, l, w1, b1, w2, b2, n_tiles):
    d = D_MODEL
    f2 = w1.shape[-1]
    ys_rows = n_tiles * TILE_ROWS + EXPERT_BLOCK
    expert_block = lambda i, be, ne, nb, nr, s, t: (l, be[i], 0, 0)
    grid_spec = pltpu.PrefetchScalarGridSpec(
        num_scalar_prefetch=6,
        grid=(N_BLOCKS,),
        in_specs=[
            pl.BlockSpec(memory_space=pl.ANY),
            pl.BlockSpec(memory_space=pl.ANY),
            pl.BlockSpec((1, 1, 1, f2), expert_block),
            pl.BlockSpec(memory_space=pl.ANY),
            pl.BlockSpec((1, 1, 1, d), expert_block),
        ],
        out_specs=pl.BlockSpec(memory_space=pl.ANY),
        scratch_shapes=[
            pltpu.VMEM((d, f2), F32),
            pltpu.VMEM((D_FF, d), F32),
            pltpu.VMEM((d, f2), BF16),
            pltpu.VMEM((D_FF, d), BF16),
            pltpu.VMEM((2, EXPERT_BLOCK, XS_WORDS), jnp.uint32),
            pltpu.VMEM((2, EXPERT_BLOCK, HALF_D), jnp.uint32),
            pltpu.SemaphoreType.DMA((2,)),
            pltpu.SemaphoreType.DMA((2,)),
            pltpu.SemaphoreType.DMA((2,)),
        ],
    )
    return pl.pallas_call(
        functools.partial(_expert_kernel, l),
        grid_spec=grid_spec,
        out_shape=jax.ShapeDtypeStruct((ys_rows, HALF_D), jnp.uint32),
        compiler_params=pltpu.CompilerParams(
            dimension_semantics=("arbitrary",), vmem_limit_bytes=VMEM_LIMIT),
        name="experts",
    )(block_e, next_e, n_used, n_real, src, dst, xs, w1, b1, w2, b2)


def _combine_kernel(ys_ref, meta_ref, x_ref, mod_ref, g_ref, b_ref, o_ref):
    tt = x_ref.shape[0]
    slot_iota = lax.broadcasted_iota(jnp.int32, (tt, TILE_ROWS), 1).astype(F32)
    p = jnp.zeros((tt, TILE_ROWS), F32)
    for k in range(TOP_K):
        p = jnp.where(slot_iota == meta_ref[:, k:k + 1], 1.0, p)
    y = jnp.dot(p.astype(BF16), _unpack_halves(ys_ref[...]), preferred_element_type=F32)
    g2 = mod_ref[0, 0, 5:6, :]
    o_ref[...] = _ln(DEEPNORM_ALPHA * x_ref[...] + g2 * y) * g_ref[...] + b_ref[...]


def _combine(ys, meta, x1, mod, l, g, b, seq):
    n_tok, d = x1.shape
    tt = TOK_TILE
    tiles_per_seq = seq // tt
    return pl.pallas_call(
        _combine_kernel,
        grid=(n_tok // tt,),
        in_specs=[
            pl.BlockSpec((TILE_ROWS, HALF_D), lambda i: (i, 0)),
            pl.BlockSpec((tt, LANES), lambda i: (i, 0)),
            pl.BlockSpec((tt, d), lambda i: (i, 0)),
            pl.BlockSpec((1, 1, 6, d), lambda i: (l, i // tiles_per_seq, 0, 0)),
            pl.BlockSpec((1, d), lambda i: (0, 0)),
            pl.BlockSpec((1, d), lambda i: (0, 0)),
        ],
        out_specs=pl.BlockSpec((tt, d), lambda i: (i, 0)),
        out_shape=jax.ShapeDtypeStruct((n_tok, d), F32),
        compiler_params=pltpu.CompilerParams(
            dimension_semantics=("arbitrary",), vmem_limit_bytes=VMEM_LIMIT),
        name="combine",
    )(ys, meta, x1, mod, g, b)


def kernel(x, c, ada_w, ada_b, w_in, b_in, conv_a_w, conv_a_b, ln_a_g, ln_a_b, conv_b_w, w_pa, b_pa,
           w_pb, w_o, ln1_g, ln1_b, router_w, router_b, w1, b1, w2, b2, ln2_g, ln2_b):
    bsz, seq, d = x.shape
    n_layers = ada_w.shape[0]
    n_tok = bsz * seq
    n_tiles = n_tok // TOK_TILE
    assert (n_tok, d) == (N_TOKENS, D_MODEL)

    mod = _ada_mod(c, ada_w, ada_b)

    w_in_b = w_in.astype(BF16)
    w_pa_b = w_pa.astype(BF16)
    w_pb_b = w_pb.astype(BF16)
    w_o_b = w_o.astype(BF16)
    caw = jnp.repeat(conv_a_w, SUBLANES, axis=1)
    cbw = jnp.repeat(conv_b_w, SUBLANES, axis=1)
    rw_pad = jnp.pad(router_w, ((0, 0), (0, 0), (0, LANES - N_EXPERTS)))
    rw_hi = rw_pad.astype(BF16)
    rw_lo = (rw_pad - rw_hi.astype(F32)).astype(BF16)
    rb_pad = jnp.pad(router_b, ((0, 0), (0, LANES - N_EXPERTS)))[:, None, :]
    b1r = b1.reshape(n_layers, N_EXPERTS, 1, 2 * D_FF)
    b2r = b2.reshape(n_layers, N_EXPERTS, 1, d)
    row = lambda a, l: a[l][None, :]
    rows = lambda a: a[:, None, :]

    for l in range(n_layers):
        x = _token_mixer(x, mod, l, w_in_b, rows(b_in), caw, rows(conv_a_b), rows(ln_a_g), rows(ln_a_b),
                         cbw, w_pa_b, rows(b_pa), w_pb_b, w_o_b, rows(ln1_g), rows(ln1_b))
        x1 = x.reshape(n_tok, d)
        xs, meta, tab = _router(x1, mod, l, rw_hi, rw_lo, rb_pad, seq)
        block_e, next_e, n_used, n_real, src, dst = _granule_plan(tab, n_tiles)
        ys = _experts(block_e, next_e, n_used, n_real, src, dst, xs, l, w1, b1r, w2, b2r, n_tiles)
        x = _combine(ys, meta, x1, mod, l, row(ln2_g, l), row(ln2_b, l), seq)
        x = x.reshape(bsz, seq, d)
    return x
```

```python
import functools

import jax
import jax.numpy as jnp
from jax import lax
from jax.experimental import pallas as pl
from jax.experimental.pallas import tpu as pltpu

D_MODEL = 1024
DEPTH = 4
CONV_A = 31
CONV_B = 3
N_EXPERTS = 32
TOP_K = 4
D_FF = D_MODEL
SWIGLU_LIMIT = 7.0
SWIGLU_ALPHA = 1.702
EXPERT_BLOCK = 256
DEEPNORM_ALPHA = (2.0 * DEPTH) ** 0.25
LN_EPS = 1e-5

LANES = 128
SUBLANES = 8
VMEM_LIMIT = 60 * 1024 * 1024

MIX_TS = 512
CONV_RC = 64
CONV_LC = 256
PROJ_NC = 512
CARRY_A = 32
CARRY_B = 8
TOK_TILE = 256
GRANULE = SUBLANES
BLOCK_GRANULES = EXPERT_BLOCK // GRANULE
N_TOKENS = 16384
TILE_ROWS = 1280
HALF_D = D_MODEL // 2
XS_WORDS = HALF_D + LANES
assert TILE_ROWS >= TOP_K * TOK_TILE + N_EXPERTS * (GRANULE - 1) and TILE_ROWS % LANES == 0
N_BLOCKS = -(-((N_TOKENS // TOK_TILE) * (TILE_ROWS // GRANULE)
               + (N_EXPERTS + 1) * (BLOCK_GRANULES - 1)) // BLOCK_GRANULES)

F32 = jnp.float32
BF16 = jnp.bfloat16


def _ln(x):
    mu = jnp.mean(x, axis=-1, keepdims=True)
    xc = x - mu
    var = jnp.mean(xc * xc, axis=-1, keepdims=True)
    return xc * lax.rsqrt(var + LN_EPS)


NEG_LOG2E = -1.4426950408889634


def _sigmoid(x, scale=1.0):
    return 1.0 / (1.0 + jnp.exp2(x * (scale * NEG_LOG2E)))


def _bdot(a, b):
    return jnp.dot(a.astype(BF16), b, preferred_element_type=F32)


HIGH_HALF = 0xFFFF0000


def _pack_halves(x):
    h = x.shape[1] // 2
    lo = lax.shift_right_logical(lax.bitcast_convert_type(x[:, :h], jnp.uint32), jnp.uint32(16))
    hi = lax.bitcast_convert_type(x[:, h:], jnp.uint32) & jnp.uint32(HIGH_HALF)
    return hi | lo


def _unpack_halves(w):
    lo = lax.bitcast_convert_type(lax.shift_left(w, jnp.uint32(16)), F32)
    hi = lax.bitcast_convert_type(w & jnp.uint32(HIGH_HALF), F32)
    return jnp.concatenate([lo.astype(BF16), hi.astype(BF16)], axis=1)


def _ada_kernel(c_ref, w_ref, b_ref, o_ref):
    c = c_ref[...]
    cond = c * _sigmoid(c)
    o_ref[0] = jnp.dot(cond, w_ref[0], preferred_element_type=F32,
                       precision=lax.Precision.HIGHEST) + b_ref[0]


def _ada_mod(c, ada_w, ada_b):
    bsz, d = c.shape
    n_layers = ada_w.shape[0]
    c_pad = jnp.zeros((SUBLANES, d), F32).at[:bsz].set(c)
    out = pl.pallas_call(
        _ada_kernel,
        grid=(n_layers, 6),
        in_specs=[
            pl.BlockSpec((SUBLANES, d), lambda l, j: (0, 0)),
            pl.BlockSpec((1, d, d), lambda l, j: (l, 0, j)),
            pl.BlockSpec((1, 1, d), lambda l, j: (l, 0, j)),
        ],
        out_specs=pl.BlockSpec((1, SUBLANES, d), lambda l, j: (l, 0, j)),
        out_shape=jax.ShapeDtypeStruct((n_layers, SUBLANES, 6 * d), F32),
        compiler_params=pltpu.CompilerParams(
            dimension_semantics=("arbitrary", "arbitrary"), vmem_limit_bytes=VMEM_LIMIT),
        name="ada_mod",
    )(c_pad, ada_w, ada_b.reshape(n_layers, 1, 6 * d))
    return out[:, :bsz].reshape(n_layers, bsz, 6, d)


def _combine_tile(ys_words, meta, x1, g2, g, b):
    tt = x1.shape[0]
    slot_iota = lax.broadcasted_iota(jnp.int32, (tt, TILE_ROWS), 1).astype(F32)
    p = jnp.zeros((tt, TILE_ROWS), F32)
    for k in range(TOP_K):
        p = jnp.where(slot_iota == meta[:, k:k + 1], 1.0, p)
    y = jnp.dot(p.astype(BF16), _unpack_halves(ys_words), preferred_element_type=F32)
    return _ln(DEEPNORM_ALPHA * x1 + g2 * y) * g + b


def _conv_chunk(ext_ref, w_ref, out_ref, n_taps, carry_rows, r0, c0, dep=None):
    base = carry_rows - n_taps + 1
    out = None
    for b in range(SUBLANES):
        win = CONV_RC if b == 0 else CONV_RC + SUBLANES
        z = None
        for k in range(n_taps):
            if (base + k) % SUBLANES != b:
                continue
            a8 = base + k - b
            w8 = w_ref[pl.ds(SUBLANES * k, SUBLANES), c0:c0 + CONV_LC]
            term = jnp.tile(w8, (win // SUBLANES, 1)) * ext_ref[pl.ds(r0 + a8, win), c0:c0 + CONV_LC]
            z = term if z is None else z + term
        if z is None:
            continue
        if b != 0:
            z = pltpu.roll(z, win - b, axis=0)[0:CONV_RC]
        out = z if out is None else out + z
    if dep is not None:
        out = out + jnp.tile(dep, (CONV_RC // SUBLANES, CONV_LC // LANES))
    out_ref[pl.ds(r0, CONV_RC), c0:c0 + CONV_LC] = out


def _mixer_kernel(fused, *refs):
    if fused:
        ys_ref, meta_ref, x1_ref, mod_prev_ref, g2_ref, b2_ref = refs[:6]
        refs, xin = refs[6:-1], refs[-1]
    else:
        x_ref, refs = refs[0], refs[1:]
    (mod_ref, w_in_ref, b_in_ref, caw_ref, cab_ref, lag_ref, lab_ref, cbw_ref, w_pa_ref, b_pa_ref,
     w_pb_ref, w_o_ref, g_ref, b_ref, o_ref, ext_a, ext_b, cv_a, cv_b, gb_s, sg_s) = refs
    ts = o_ref.shape[1]
    d = D_MODEL

    @pl.when(pl.program_id(1) == 0)
    def _():
        ext_a[pl.ds(0, CARRY_A), :] = jnp.zeros((CARRY_A, d), F32)
        ext_b[pl.ds(0, CARRY_B), :] = jnp.zeros((CARRY_B, d), F32)

    if fused:
        for t in range(ts // TOK_TILE):
            rows = pl.ds(t * TOK_TILE, TOK_TILE)
            xin[rows, :] = _combine_tile(
                ys_ref[pl.ds(t * TILE_ROWS, TILE_ROWS), :], meta_ref[rows, :], x1_ref[0, rows, :],
                mod_prev_ref[0, 0, 5:6, :], g2_ref[...], b2_ref[...])
        xt = xin[...]
    else:
        xt = x_ref[0]
    sh1 = mod_ref[0, 0, 0:1, :]
    sc1 = mod_ref[0, 0, 1:2, :]
    g1 = mod_ref[0, 0, 2:3, :]
    u = (_ln(xt) * (1.0 + sc1) + sh1).astype(BF16)

    def proj(c0, n):
        return jnp.dot(u, w_in_ref[:, c0:c0 + n], preferred_element_type=F32) + b_in_ref[:, c0:c0 + n]

    cur_a = pl.ds(CARRY_A, ts)
    cur_b = pl.ds(CARRY_B, ts)
    nc = PROJ_NC

    def tie(z):
        bits = lax.bitcast_convert_type(z[0:SUBLANES, 0:LANES], jnp.uint32)
        zero = lax.shift_right_logical(lax.shift_right_logical(bits, jnp.uint32(16)), jnp.uint32(16))
        return lax.bitcast_convert_type(zero, F32)

    za = proj(0, 2 * d)
    ext_a[cur_a, :] = za[:, 0:d] * _sigmoid(za[:, d:2 * d])

    def b_gate(c):
        z = proj(2 * d + c, nc)
        gb_s[:, c:c + nc] = z
        return tie(z)

    def b_c(c):
        z = proj(3 * d + c, nc)
        ext_b[cur_b, c:c + nc] = z
        return tie(z)

    def b_h(c):
        z = proj(4 * d + c, nc)
        ext_b[cur_b, c:c + nc] = ext_b[cur_b, c:c + nc] * z
        return tie(z)

    def merge_gate(c):
        z = proj(5 * d + c, nc)
        sg_s[:, c:c + nc] = _sigmoid(z)
        return tie(z)

    jobs = [lambda c=c, f=f: f(c) for f in (b_gate, b_c, b_h) for c in range(0, d, nc)]
    jobs += [lambda c=c: merge_gate(c) for c in range(0, 2 * d, nc)]
    conv_chunks = [(r0, c0) for r0 in range(0, ts, CONV_RC) for c0 in range(0, d, CONV_LC)]
    assert len(jobs) < len(conv_chunks)
    stride = (len(conv_chunks) - 1) // len(jobs)
    dep = None
    for i, (r0, c0) in enumerate(conv_chunks):
        nxt = jobs[i // stride]() if i % stride == 0 and i // stride < len(jobs) else None
        _conv_chunk(ext_a, caw_ref, cv_a, CONV_A, CARRY_A, r0, c0, dep)
        dep = nxt
    for r0, c0 in conv_chunks:
        _conv_chunk(ext_b, cbw_ref, cv_b, CONV_B, CARRY_B, r0, c0)

    ext_a[pl.ds(0, CARRY_A), :] = ext_a[pl.ds(ts, CARRY_A), :]
    ext_b[pl.ds(0, CARRY_B), :] = ext_b[pl.ds(ts, CARRY_B), :]

    ya = _ln(cv_a[...] + cab_ref[...]) * lag_ref[...] + lab_ref[...]
    ya = ya * _sigmoid(ya)
    ya = _bdot(ya, w_pa_ref[...]) + b_pa_ref[...]
    yb = _bdot(gb_s[...] * cv_b[...], w_pb_ref[...])

    m = sg_s[:, 0:d] * ya + sg_s[:, d:2 * d] * yb
    y = _bdot(m, w_o_ref[...])
    o_ref[0] = _ln(DEEPNORM_ALPHA * xt + g1 * y) * g_ref[...] + b_ref[...]


def _layer_spec(l, shape):
    nd = len(shape)
    return pl.BlockSpec((None,) + shape, lambda b, s: (l,) + (0,) * nd, pipeline_mode=pl.Buffered(1))


def _token_mixer(x, mod, l, w_in, b_in, caw, cab, lag, lab, cbw, w_pa, b_pa, w_pb, w_o, g, b, prev=None):
    fused = prev is not None
    bsz, seq, d = prev[2].shape if fused else x.shape
    ts = MIX_TS
    n_in = w_in.shape[-1]
    tile_spec = pl.BlockSpec((1, ts, d), lambda b, s: (b, s, 0))
    if fused:
        steps = seq // ts
        lead_specs = [
            pl.BlockSpec((ts // TOK_TILE * TILE_ROWS, HALF_D), lambda b, s: (b * steps + s, 0)),
            pl.BlockSpec((ts, LANES), lambda b, s: (b * steps + s, 0)),
            tile_spec,
            pl.BlockSpec((1, 1, 6, d), lambda b, s: (l - 1, b, 0, 0)),
            _layer_spec(l - 1, (1, d)),
            _layer_spec(l - 1, (1, d)),
        ]
        lead_args = (prev[0], prev[1], prev[2], mod, prev[3], prev[4])
    else:
        lead_specs, lead_args = [tile_spec], (x,)
    return pl.pallas_call(
        functools.partial(_mixer_kernel, fused),
        grid=(bsz, seq // ts),
        in_specs=lead_specs + [
            pl.BlockSpec((1, 1, 6, d), lambda b, s: (l, b, 0, 0)),
            _layer_spec(l, (d, n_in)),
            _layer_spec(l, (1, n_in)),
            _layer_spec(l, (SUBLANES * CONV_A, d)),
            _layer_spec(l, (1, d)),
            _layer_spec(l, (1, d)),
            _layer_spec(l, (1, d)),
            _layer_spec(l, (SUBLANES * CONV_B, d)),
            _layer_spec(l, (d, d)),
            _layer_spec(l, (1, d)),
            _layer_spec(l, (d, d)),
            _layer_spec(l, (d, d)),
            _layer_spec(l, (1, d)),
            _layer_spec(l, (1, d)),
        ],
        out_specs=tile_spec,
        out_shape=jax.ShapeDtypeStruct((bsz, seq, d), F32),
        scratch_shapes=[
            pltpu.VMEM((ts + CARRY_A, d), F32),
            pltpu.VMEM((ts + CARRY_B, d), F32),
            pltpu.VMEM((ts, d), F32),
            pltpu.VMEM((ts, d), F32),
            pltpu.VMEM((ts, d), F32),
            pltpu.VMEM((ts, 2 * d), F32),
        ] + ([pltpu.VMEM((ts, d), F32)] if fused else []),
        compiler_params=pltpu.CompilerParams(
            dimension_semantics=("arbitrary", "arbitrary"), vmem_limit_bytes=VMEM_LIMIT),
        name="token_mixer",
    )(*lead_args, mod, w_in, b_in, caw, cab, lag, lab, cbw, w_pa, b_pa, w_pb, w_o, g, b)


def _router_kernel(x_ref, mod_ref, rwh_ref, rwl_ref, rb_ref, xs_ref, meta_ref, tab_ref):
    tt = x_ref.shape[0]
    d = D_MODEL

    sh2 = mod_ref[0, 0, 3:4, :]
    sc2 = mod_ref[0, 0, 4:5, :]
    u2 = _ln(x_ref[...]) * (1.0 + sc2) + sh2
    u2h = u2.astype(BF16)
    u2l = (u2 - u2h.astype(F32)).astype(BF16)
    logits = (jnp.dot(u2h, rwh_ref[...], preferred_element_type=F32)
              + jnp.dot(u2l, rwh_ref[...], preferred_element_type=F32)
              + jnp.dot(u2h, rwl_ref[...], preferred_element_type=F32)) + rb_ref[...]
    lane = lax.broadcasted_iota(jnp.int32, (tt, LANES), 1)
    lane_f = lane.astype(F32)
    neg_inf = jnp.float32(-jnp.inf)
    work = jnp.where(lane < N_EXPERTS, logits, neg_inf)
    sel = jnp.zeros((tt, LANES), F32)
    vals, idxs = [], []
    for _ in range(TOP_K):
        m = jnp.max(work, axis=-1, keepdims=True)
        idx = jnp.min(jnp.where(work == m, lane_f, float(LANES)), axis=-1, keepdims=True)
        hit = lane_f == idx
        vals.append(m)
        idxs.append(idx)
        work = jnp.where(hit, neg_inf, work)
        sel = sel + hit.astype(F32)
    exps = [jnp.exp(v - vals[0]) for v in vals]
    inv = 1.0 / (exps[0] + exps[1] + exps[2] + exps[3])

    n_e = jnp.sum(sel, axis=0, keepdims=True)
    c8 = jnp.floor((n_e + float(GRANULE - 1)) * (1.0 / GRANULE))
    er = lax.broadcasted_iota(jnp.int32, (LANES, LANES), 0)
    ec = lax.broadcasted_iota(jnp.int32, (LANES, LANES), 1)
    upper = (er < ec).astype(BF16)
    off8 = jnp.dot(jnp.broadcast_to(c8, (SUBLANES, LANES)).astype(BF16), upper,
                   preferred_element_type=F32)[0:1]
    row = lax.broadcasted_iota(jnp.int32, (tt, tt), 0)
    col = lax.broadcasted_iota(jnp.int32, (tt, tt), 1)
    tri = (col < row).astype(BF16)
    prefix = jnp.dot(tri, sel.astype(BF16), preferred_element_type=F32)
    slot_dense = off8 * float(GRANULE) + prefix

    meta = jnp.zeros((tt, LANES), F32)
    for k in range(TOP_K):
        pos_k = jnp.sum(jnp.where(lane_f == idxs[k], slot_dense, 0.0), axis=-1, keepdims=True)
        meta = jnp.where(lane == k, pos_k, meta)
        meta = jnp.where(lane == TOP_K + k, exps[k] * inv, meta)
    meta_ref[...] = meta

    sub = lax.broadcasted_iota(jnp.int32, (SUBLANES, LANES), 0)
    tab_ref[0] = jnp.where(sub == 0, c8, jnp.where(sub == 1, off8, 0.0))

    meta_t = meta.T
    slot_iota = lax.broadcasted_iota(jnp.int32, (TILE_ROWS, tt), 0).astype(F32)
    pt = jnp.zeros((TILE_ROWS, tt), F32)
    gm = jnp.zeros((TILE_ROWS, tt), F32)
    for k in range(TOP_K):
        hit = slot_iota == meta_t[k:k + 1, :]
        pt = jnp.where(hit, 1.0, pt)
        gm = jnp.where(hit, meta_t[TOP_K + k:TOP_K + k + 1, :], gm)
    xs_ref[:, 0:HALF_D] = _pack_halves(jnp.dot(pt.astype(BF16), u2h, preferred_element_type=F32))
    gate_row = jnp.sum(gm, axis=-1, keepdims=True)
    xs_ref[:, HALF_D:XS_WORDS] = lax.bitcast_convert_type(
        jnp.broadcast_to(gate_row, (TILE_ROWS, LANES)), jnp.uint32)


def _router(x1, mod, l, rw_hi, rw_lo, rb_pad, seq):
    n_tok, d = x1.shape
    tt = TOK_TILE
    n_tiles = n_tok // tt
    tiles_per_seq = seq // tt
    return pl.pallas_call(
        _router_kernel,
        grid=(n_tiles,),
        in_specs=[
            pl.BlockSpec((tt, d), lambda i: (i, 0)),
            pl.BlockSpec((1, 1, 6, d), lambda i: (l, i // tiles_per_seq, 0, 0)),
            pl.BlockSpec((None, d, LANES), lambda i: (l, 0, 0)),
            pl.BlockSpec((None, d, LANES), lambda i: (l, 0, 0)),
            pl.BlockSpec((None, 1, LANES), lambda i: (l, 0, 0)),
        ],
        out_specs=[
            pl.BlockSpec((TILE_ROWS, XS_WORDS), lambda i: (i, 0)),
            pl.BlockSpec((tt, LANES), lambda i: (i, 0)),
            pl.BlockSpec((1, SUBLANES, LANES), lambda i: (i, 0, 0)),
        ],
        out_shape=[
            jax.ShapeDtypeStruct((n_tiles * TILE_ROWS, XS_WORDS), jnp.uint32),
            jax.ShapeDtypeStruct((n_tok, LANES), F32),
            jax.ShapeDtypeStruct((n_tiles, SUBLANES, LANES), F32),
        ],
        compiler_params=pltpu.CompilerParams(
            dimension_semantics=("arbitrary",), vmem_limit_bytes=VMEM_LIMIT),
        name="router",
    )(x1, mod, rw_hi, rw_lo, rb_pad)


def _granule_plan(tab, n_tiles):
    c8 = tab[:, 0, :N_EXPERTS].astype(jnp.int32).T
    off8 = tab[:, 1, :N_EXPERTS].astype(jnp.int32).T
    used = jnp.sum(c8, axis=0, keepdims=True)
    c8 = jnp.concatenate([c8, TILE_ROWS // GRANULE - used], axis=0)
    off8 = jnp.concatenate([off8, used], axis=0)
    per_e = jnp.sum(c8, axis=1)
    padded = (per_e + BLOCK_GRANULES - 1) // BLOCK_GRANULES * BLOCK_GRANULES
    e_end = jnp.cumsum(padded)
    e_start = e_end - padded
    strip_start = (e_start[:, None] + jnp.cumsum(c8, axis=1) - c8).reshape(-1)
    cnt_flat = c8.reshape(-1)
    off_flat = off8.reshape(-1)
    slot = jnp.arange(N_BLOCKS * BLOCK_GRANULES, dtype=jnp.int32)
    before = strip_start[None, :] <= slot[:, None]
    tile_flat = jnp.arange(c8.size, dtype=jnp.int32) % n_tiles

    def at_strip(table):
        diff = table - jnp.concatenate([jnp.zeros((1,), jnp.int32), table[:-1]])
        return jnp.sum(jnp.where(before, diff[None, :], 0), axis=1)

    g = slot - at_strip(strip_start)
    valid = g < at_strip(cnt_flat)
    rows = at_strip(tile_flat * TILE_ROWS + off_flat * GRANULE) + g * GRANULE
    src = jnp.where(valid, rows, 0)
    trash = n_tiles * TILE_ROWS + (slot % BLOCK_GRANULES) * GRANULE
    dst = jnp.where(valid, rows, trash)
    block_start = jnp.arange(N_BLOCKS, dtype=jnp.int32) * BLOCK_GRANULES
    block_e = jnp.minimum(
        jnp.sum((e_end[None, :] <= block_start[:, None]).astype(jnp.int32), axis=1), N_EXPERTS - 1)
    n_used = e_end[-1:] // BLOCK_GRANULES
    n_real = e_end[N_EXPERTS - 1:N_EXPERTS] // BLOCK_GRANULES
    end_blk = e_end[:N_EXPERTS] // BLOCK_GRANULES
    expert_ids = jnp.arange(N_EXPERTS, dtype=jnp.int32)
    run_end = jnp.sum(jnp.where(block_e[:, None] == expert_ids[None, :], end_blk[None, :], 0), axis=1)
    after = jnp.minimum(jnp.sum((end_blk[None, :] <= run_end[:, None]).astype(jnp.int32), axis=1), N_EXPERTS - 1)
    next_e = jnp.where(run_end < n_real, after, -1)
    i32 = lambda a: a.astype(jnp.int32)
    return i32(block_e), i32(next_e), i32(n_used), i32(n_real), i32(src), i32(dst)


def _granule_copy(hbm_ref, row, buf_ref, slot, i, sem, to_hbm):
    hbm = hbm_ref.at[pl.ds(pl.multiple_of(row, GRANULE), GRANULE)]
    vmem = buf_ref.at[slot, pl.ds(i * GRANULE, GRANULE)]
    return pltpu.make_async_copy(vmem, hbm, sem) if to_hbm else pltpu.make_async_copy(hbm, vmem, sem)


def _block_copy(hbm_ref, buf_ref, slot, sem, to_hbm):
    hbm = hbm_ref.at[pl.ds(0, EXPERT_BLOCK)]
    vmem = buf_ref.at[slot]
    return pltpu.make_async_copy(vmem, hbm, sem) if to_hbm else pltpu.make_async_copy(hbm, vmem, sem)


def _expert_kernel(layer, be_ref, ne_ref, nb_ref, nr_ref, src_ref, dst_ref, xs_ref, w1_ref, b1_ref, w2_ref,
                   b2_ref, ys_ref, w1s, w2s, w1b, w2b, xbuf, ybuf, in_sem, out_sem, w_sem):
    b = pl.program_id(0)
    n_used = nb_ref[0]
    n_real = nr_ref[0]
    e = be_ref[b]
    prev = be_ref[jnp.maximum(b - 1, 0)]
    slot = b % 2

    def start_gather(blk, s):
        for i in range(BLOCK_GRANULES):
            _granule_copy(xs_ref, src_ref[blk * BLOCK_GRANULES + i], xbuf, s, i, in_sem.at[s], False).start()

    def weight_copies(expert):
        return (pltpu.make_async_copy(w1_ref.at[layer, expert], w1s, w_sem.at[0]),
                pltpu.make_async_copy(w2_ref.at[layer, expert], w2s, w_sem.at[1]))

    @pl.when(b == 0)
    def _():
        for cp in weight_copies(e):
            cp.start()

    @pl.when(jnp.logical_and(b < n_real, jnp.logical_or(b == 0, e != prev)))
    def _():
        for cp in weight_copies(e):
            cp.wait()
        w1b[...] = w1s[...].astype(BF16)
        w2b[...] = w2s[...].astype(BF16)

        @pl.when(ne_ref[b] >= 0)
        def _():
            for cp in weight_copies(ne_ref[b]):
                cp.start()

    @pl.when(b == 0)
    def _():
        start_gather(0, 0)
        ybuf[1] = jnp.zeros(ybuf.shape[1:], jnp.uint32)
        pad_rows = ys_ref.at[pl.ds(ys_ref.shape[0] - EXPERT_BLOCK, EXPERT_BLOCK)]
        fill = pltpu.make_async_copy(ybuf.at[1], pad_rows, out_sem.at[1])
        fill.start()
        fill.wait()

    @pl.when(b + 1 < n_used)
    def _():
        start_gather(b + 1, 1 - slot)

    @pl.when(b < n_used)
    def _():
        _block_copy(xs_ref, xbuf, slot, in_sem.at[slot], False).wait()

        @pl.when(b >= 2)
        def _():
            _block_copy(ys_ref, ybuf, slot, out_sem.at[slot], True).wait()

        @pl.when(b < n_real)
        def _():
            xw = xbuf[slot]
            gate = lax.bitcast_convert_type(xw[:, HALF_D:HALF_D + 1], F32)
            h = jnp.dot(_unpack_halves(xw[:, 0:HALF_D]), w1b[...], preferred_element_type=F32) + b1_ref[0, 0]
            h_glu = jnp.minimum(h[:, 0:D_FF], SWIGLU_LIMIT)
            h_lin = jnp.clip(h[:, D_FF:2 * D_FF], -SWIGLU_LIMIT, SWIGLU_LIMIT)
            a = h_glu * _sigmoid(h_glu, SWIGLU_ALPHA) * (h_lin + 1.0)
            y = (_bdot(a, w2b[...]) + b2_ref[0, 0]) * gate
            ybuf[slot] = _pack_halves(y.astype(BF16).astype(F32))

        @pl.when(b >= n_real)
        def _():
            ybuf[slot] = jnp.zeros(ybuf.shape[1:], jnp.uint32)

        for i in range(BLOCK_GRANULES):
            _granule_copy(ys_ref, dst_ref[b * BLOCK_GRANULES + i], ybuf, slot, i, out_sem.at[slot], True).start()

    @pl.when(b == n_used - 1)
    def _():
        @pl.when(b >= 1)
        def _():
            _block_copy(ys_ref, ybuf, 1 - slot, out_sem.at[1 - slot], True).wait()

        _block_copy(ys_ref, ybuf, slot, out_sem.at[slot], True).wait()


def _experts(block_e, next_e, n_used, n_real, src, dst, xs, l, w1, b1, w2, b2, n_tiles):
    d = D_MODEL
    f2 = w1.shape[-1]
    ys_rows = n_tiles * TILE_ROWS + EXPERT_BLOCK
    expert_block = lambda i, be, ne, nb, nr, s, t: (l, be[i], 0, 0)
    grid_spec = pltpu.PrefetchScalarGridSpec(
        num_scalar_prefetch=6,
        grid=(N_BLOCKS,),
        in_specs=[
            pl.BlockSpec(memory_space=pl.ANY),
            pl.BlockSpec(memory_space=pl.ANY),
            pl.BlockSpec((1, 1, 1, f2), expert_block),
            pl.BlockSpec(memory_space=pl.ANY),
            pl.BlockSpec((1, 1, 1, d), expert_block),
        ],
        out_specs=pl.BlockSpec(memory_space=pl.ANY),
        scratch_shapes=[
            pltpu.VMEM((d, f2), F32),
            pltpu.VMEM((D_FF, d), F32),
            pltpu.VMEM((d, f2), BF16),
            pltpu.VMEM((D_FF, d), BF16),
            pltpu.VMEM((2, EXPERT_BLOCK, XS_WORDS), jnp.uint32),
            pltpu.VMEM((2, EXPERT_BLOCK, HALF_D), jnp.uint32),
            pltpu.SemaphoreType.DMA((2,)),
            pltpu.SemaphoreType.DMA((2,)),
            pltpu.SemaphoreType.DMA((2,)),
        ],
    )
    return pl.pallas_call(
        functools.partial(_expert_kernel, l),
        grid_spec=grid_spec,
        out_shape=jax.ShapeDtypeStruct((ys_rows, HALF_D), jnp.uint32),
        compiler_params=pltpu.CompilerParams(
            dimension_semantics=("arbitrary",), vmem_limit_bytes=VMEM_LIMIT),
        name="experts",
    )(block_e, next_e, n_used, n_real, src, dst, xs, w1, b1, w2, b2)


def _combine_kernel(ys_ref, meta_ref, x_ref, mod_ref, g_ref, b_ref, o_ref):
    o_ref[...] = _combine_tile(ys_ref[...], meta_ref[...], x_ref[...], mod_ref[0, 0, 5:6, :], g_ref[...], b_ref[...])


def _combine(ys, meta, x1, mod, l, g, b, seq):
    n_tok, d = x1.shape
    tt = TOK_TILE
    tiles_per_seq = seq // tt
    return pl.pallas_call(
        _combine_kernel,
        grid=(n_tok // tt,),
        in_specs=[
            pl.BlockSpec((TILE_ROWS, HALF_D), lambda i: (i, 0)),
            pl.BlockSpec((tt, LANES), lambda i: (i, 0)),
            pl.BlockSpec((tt, d), lambda i: (i, 0)),
            pl.BlockSpec((1, 1, 6, d), lambda i: (l, i // tiles_per_seq, 0, 0)),
            pl.BlockSpec((1, d), lambda i: (0, 0)),
            pl.BlockSpec((1, d), lambda i: (0, 0)),
        ],
        out_specs=pl.BlockSpec((tt, d), lambda i: (i, 0)),
        out_shape=jax.ShapeDtypeStruct((n_tok, d), F32),
        compiler_params=pltpu.CompilerParams(
            dimension_semantics=("arbitrary",), vmem_limit_bytes=VMEM_LIMIT),
        name="combine",
    )(ys, meta, x1, mod, g, b)


def kernel(x, c, ada_w, ada_b, w_in, b_in, conv_a_w, conv_a_b, ln_a_g, ln_a_b, conv_b_w, w_pa, b_pa,
           w_pb, w_o, ln1_g, ln1_b, router_w, router_b, w1, b1, w2, b2, ln2_g, ln2_b):
    bsz, seq, d = x.shape
    n_layers = ada_w.shape[0]
    n_tok = bsz * seq
    n_tiles = n_tok // TOK_TILE
    assert (n_tok, d) == (N_TOKENS, D_MODEL)

    mod = _ada_mod(c, ada_w, ada_b)

    w_in_b = w_in.astype(BF16)
    w_pa_b = w_pa.astype(BF16)
    w_pb_b = w_pb.astype(BF16)
    w_o_b = w_o.astype(BF16)
    caw = jnp.repeat(conv_a_w, SUBLANES, axis=1)
    cbw = jnp.repeat(conv_b_w, SUBLANES, axis=1)
    rw_pad = jnp.pad(router_w, ((0, 0), (0, 0), (0, LANES - N_EXPERTS)))
    rw_hi = rw_pad.astype(BF16)
    rw_lo = (rw_pad - rw_hi.astype(F32)).astype(BF16)
    rb_pad = jnp.pad(router_b, ((0, 0), (0, LANES - N_EXPERTS)))[:, None, :]
    b1r = b1.reshape(n_layers, N_EXPERTS, 1, 2 * D_FF)
    b2r = b2.reshape(n_layers, N_EXPERTS, 1, d)
    row = lambda a, l: a[l][None, :]
    rows = lambda a: a[:, None, :]

    prev = None
    for l in range(n_layers):
        x = _token_mixer(x, mod, l, w_in_b, rows(b_in), caw, rows(conv_a_b), rows(ln_a_g), rows(ln_a_b),
                         cbw, w_pa_b, rows(b_pa), w_pb_b, w_o_b, rows(ln1_g), rows(ln1_b), prev)
        x1 = x.reshape(n_tok, d)
        xs, meta, tab = _router(x1, mod, l, rw_hi, rw_lo, rb_pad, seq)
        block_e, next_e, n_used, n_real, src, dst = _granule_plan(tab, n_tiles)
        ys = _experts(block_e, next_e, n_used, n_real, src, dst, xs, l, w1, b1r, w2, b2r, n_tiles)
        prev = (ys, meta, x, rows(ln2_g), rows(ln2_b))
        x = None
    l = n_layers - 1
    x = _combine(ys, meta, x1, mod, l, row(ln2_g, l), row(ln2_b, l), seq)
    return x.reshape(bsz, seq, d)
```

```python
import functools

import jax
import jax.numpy as jnp
from jax import lax
from jax.experimental import pallas as pl
from jax.experimental.pallas import tpu as pltpu

D_MODEL = 1024
DEPTH = 4
CONV_A = 31
CONV_B = 3
N_EXPERTS = 32
TOP_K = 4
D_FF = D_MODEL
SWIGLU_LIMIT = 7.0
SWIGLU_ALPHA = 1.702
EXPERT_BLOCK = 256
DEEPNORM_ALPHA = (2.0 * DEPTH) ** 0.25
LN_EPS = 1e-5

LANES = 128
SUBLANES = 8
VMEM_LIMIT = 60 * 1024 * 1024

MIX_TS = 512
CONV_RC = 64
CONV_LC = 256
PROJ_NC = 512
CARRY_A = 32
CARRY_B = 8
TOK_TILE = 256
GRANULE = SUBLANES
BLOCK_GRANULES = EXPERT_BLOCK // GRANULE
N_TOKENS = 16384
TILE_ROWS = 1280
HALF_D = D_MODEL // 2
XS_WORDS = HALF_D + LANES
assert TILE_ROWS >= TOP_K * TOK_TILE + N_EXPERTS * (GRANULE - 1) and TILE_ROWS % LANES == 0
N_BLOCKS = -(-((N_TOKENS // TOK_TILE) * (TILE_ROWS // GRANULE)
               + (N_EXPERTS + 1) * (BLOCK_GRANULES - 1)) // BLOCK_GRANULES)

F32 = jnp.float32
BF16 = jnp.bfloat16


def _ln(x):
    mu = jnp.mean(x, axis=-1, keepdims=True)
    xc = x - mu
    var = jnp.mean(xc * xc, axis=-1, keepdims=True)
    return xc * lax.rsqrt(var + LN_EPS)


NEG_LOG2E = -1.4426950408889634


def _sigmoid(x, scale=1.0):
    return 1.0 / (1.0 + jnp.exp2(x * (scale * NEG_LOG2E)))


def _bdot(a, b):
    return jnp.dot(a.astype(BF16), b, preferred_element_type=F32)


HIGH_HALF = 0xFFFF0000


def _pack_halves(x):
    h = x.shape[1] // 2
    lo = lax.shift_right_logical(lax.bitcast_convert_type(x[:, :h], jnp.uint32), jnp.uint32(16))
    hi = lax.bitcast_convert_type(x[:, h:], jnp.uint32) & jnp.uint32(HIGH_HALF)
    return hi | lo


def _unpack_halves(w):
    lo = lax.bitcast_convert_type(lax.shift_left(w, jnp.uint32(16)), F32)
    hi = lax.bitcast_convert_type(w & jnp.uint32(HIGH_HALF), F32)
    return jnp.concatenate([lo.astype(BF16), hi.astype(BF16)], axis=1)


def _ada_kernel(c_ref, w_ref, b_ref, o_ref):
    c = c_ref[...]
    cond = c * _sigmoid(c)
    o_ref[0] = jnp.dot(cond, w_ref[0], preferred_element_type=F32,
                       precision=lax.Precision.HIGHEST) + b_ref[0]


def _ada_mod(c, ada_w, ada_b):
    bsz, d = c.shape
    n_layers = ada_w.shape[0]
    c_pad = jnp.zeros((SUBLANES, d), F32).at[:bsz].set(c)
    out = pl.pallas_call(
        _ada_kernel,
        grid=(n_layers, 6),
        in_specs=[
            pl.BlockSpec((SUBLANES, d), lambda l, j: (0, 0)),
            pl.BlockSpec((1, d, d), lambda l, j: (l, 0, j)),
            pl.BlockSpec((1, 1, d), lambda l, j: (l, 0, j)),
        ],
        out_specs=pl.BlockSpec((1, SUBLANES, d), lambda l, j: (l, 0, j)),
        out_shape=jax.ShapeDtypeStruct((n_layers, SUBLANES, 6 * d), F32),
        compiler_params=pltpu.CompilerParams(
            dimension_semantics=("arbitrary", "arbitrary"), vmem_limit_bytes=VMEM_LIMIT),
        name="ada_mod",
    )(c_pad, ada_w, ada_b.reshape(n_layers, 1, 6 * d))
    return out[:, :bsz].reshape(n_layers, bsz, 6, d)


def _combine_tile(ys_words, meta, x1, g2, g, b):
    tt = x1.shape[0]
    slot_iota = lax.broadcasted_iota(jnp.int32, (tt, TILE_ROWS), 1).astype(F32)
    p = jnp.zeros((tt, TILE_ROWS), F32)
    for k in range(TOP_K):
        p = jnp.where(slot_iota == meta[:, k:k + 1], 1.0, p)
    y = jnp.dot(p.astype(BF16), _unpack_halves(ys_words), preferred_element_type=F32)
    return _ln(DEEPNORM_ALPHA * x1 + g2 * y) * g + b


def _conv_chunk(ext_ref, w_ref, out_ref, n_taps, carry_rows, r0, c0, dep=None):
    base = carry_rows - n_taps + 1
    out = None
    for b in range(SUBLANES):
        win = CONV_RC if b == 0 else CONV_RC + SUBLANES
        z = None
        for k in range(n_taps):
            if (base + k) % SUBLANES != b:
                continue
            a8 = base + k - b
            w8 = w_ref[pl.ds(SUBLANES * k, SUBLANES), c0:c0 + CONV_LC]
            term = jnp.tile(w8, (win // SUBLANES, 1)) * ext_ref[pl.ds(r0 + a8, win), c0:c0 + CONV_LC]
            z = term if z is None else z + term
        if z is None:
            continue
        if b != 0:
            z = pltpu.roll(z, win - b, axis=0)[0:CONV_RC]
        out = z if out is None else out + z
    if dep is not None:
        out = out + jnp.tile(dep, (CONV_RC // SUBLANES, CONV_LC // LANES))
    out_ref[pl.ds(r0, CONV_RC), c0:c0 + CONV_LC] = out


def _mixer_kernel(fused, *refs):
    if fused:
        ys_ref, meta_ref, x1_ref, mod_prev_ref, g2_ref, b2_ref = refs[:6]
        refs, xin = refs[6:-1], refs[-1]
    else:
        x_ref, refs = refs[0], refs[1:]
    (mod_ref, w_in_ref, b_in_ref, caw_ref, cab_ref, lag_ref, lab_ref, cbw_ref, w_pa_ref, b_pa_ref,
     w_pb_ref, w_o_ref, g_ref, b_ref, o_ref, ext_a, ext_b, cv_a, cv_b, gb_s, sg_s) = refs
    ts = o_ref.shape[1]
    d = D_MODEL

    @pl.when(pl.program_id(1) == 0)
    def _():
        ext_a[pl.ds(0, CARRY_A), :] = jnp.zeros((CARRY_A, d), F32)
        ext_b[pl.ds(0, CARRY_B), :] = jnp.zeros((CARRY_B, d), F32)

    if fused:
        for t in range(ts // TOK_TILE):
            rows = pl.ds(t * TOK_TILE, TOK_TILE)
            xin[rows, :] = _combine_tile(
                ys_ref[pl.ds(t * TILE_ROWS, TILE_ROWS), :], meta_ref[rows, :], x1_ref[0, rows, :],
                mod_prev_ref[0, 0, 5:6, :], g2_ref[...], b2_ref[...])
        xt = xin[...]
    else:
        xt = x_ref[0]
    sh1 = mod_ref[0, 0, 0:1, :]
    sc1 = mod_ref[0, 0, 1:2, :]
    g1 = mod_ref[0, 0, 2:3, :]
    u = (_ln(xt) * (1.0 + sc1) + sh1).astype(BF16)

    def proj(c0, n):
        return jnp.dot(u, w_in_ref[:, c0:c0 + n], preferred_element_type=F32) + b_in_ref[:, c0:c0 + n]

    cur_a = pl.ds(CARRY_A, ts)
    cur_b = pl.ds(CARRY_B, ts)
    nc = PROJ_NC

    def tie(z):
        bits = lax.bitcast_convert_type(z[0:SUBLANES, 0:LANES], jnp.uint32)
        zero = lax.shift_right_logical(lax.shift_right_logical(bits, jnp.uint32(16)), jnp.uint32(16))
        return lax.bitcast_convert_type(zero, F32)

    za = proj(0, 2 * d)
    ext_a[cur_a, :] = za[:, 0:d] * _sigmoid(za[:, d:2 * d])

    def b_gate(c):
        z = proj(2 * d + c, nc)
        gb_s[:, c:c + nc] = z
        return tie(z)

    def b_c(c):
        z = proj(3 * d + c, nc)
        ext_b[cur_b, c:c + nc] = z
        return tie(z)

    def b_h(c):
        z = proj(4 * d + c, nc)
        ext_b[cur_b, c:c + nc] = ext_b[cur_b, c:c + nc] * z
        return tie(z)

    def merge_gate(c):
        z = proj(5 * d + c, nc)
        sg_s[:, c:c + nc] = _sigmoid(z)
        return tie(z)

    jobs = [lambda c=c, f=f: f(c) for f in (b_gate, b_c, b_h) for c in range(0, d, nc)]
    jobs += [lambda c=c: merge_gate(c) for c in range(0, 2 * d, nc)]
    conv_chunks = [(r0, c0) for r0 in range(0, ts, CONV_RC) for c0 in range(0, d, CONV_LC)]
    assert len(jobs) < len(conv_chunks)
    stride = (len(conv_chunks) - 1) // len(jobs)
    dep = None
    for i, (r0, c0) in enumerate(conv_chunks):
        nxt = jobs[i // stride]() if i % stride == 0 and i // stride < len(jobs) else None
        _conv_chunk(ext_a, caw_ref, cv_a, CONV_A, CARRY_A, r0, c0, dep)
        dep = nxt
    for r0, c0 in conv_chunks:
        _conv_chunk(ext_b, cbw_ref, cv_b, CONV_B, CARRY_B, r0, c0)

    ext_a[pl.ds(0, CARRY_A), :] = ext_a[pl.ds(ts, CARRY_A), :]
    ext_b[pl.ds(0, CARRY_B), :] = ext_b[pl.ds(ts, CARRY_B), :]

    ya = _ln(cv_a[...] + cab_ref[...]) * lag_ref[...] + lab_ref[...]
    ya = ya * _sigmoid(ya)
    ya = _bdot(ya, w_pa_ref[...]) + b_pa_ref[...]
    yb = _bdot(gb_s[...] * cv_b[...], w_pb_ref[...])

    m = sg_s[:, 0:d] * ya + sg_s[:, d:2 * d] * yb
    y = _bdot(m, w_o_ref[...])
    o_ref[0] = _ln(DEEPNORM_ALPHA * xt + g1 * y) * g_ref[...] + b_ref[...]


def _layer_spec(l, shape):
    nd = len(shape)
    return pl.BlockSpec((None,) + shape, lambda b, s: (l,) + (0,) * nd, pipeline_mode=pl.Buffered(1))


def _token_mixer(x, mod, l, w_in, b_in, caw, cab, lag, lab, cbw, w_pa, b_pa, w_pb, w_o, g, b, prev=None):
    fused = prev is not None
    bsz, seq, d = prev[2].shape if fused else x.shape
    ts = MIX_TS
    n_in = w_in.shape[-1]
    tile_spec = pl.BlockSpec((1, ts, d), lambda b, s: (b, s, 0))
    if fused:
        steps = seq // ts
        lead_specs = [
            pl.BlockSpec((ts // TOK_TILE * TILE_ROWS, HALF_D), lambda b, s: (b * steps + s, 0)),
            pl.BlockSpec((ts, LANES), lambda b, s: (b * steps + s, 0)),
            tile_spec,
            pl.BlockSpec((1, 1, 6, d), lambda b, s: (l - 1, b, 0, 0)),
            _layer_spec(l - 1, (1, d)),
            _layer_spec(l - 1, (1, d)),
        ]
        lead_args = (prev[0], prev[1], prev[2], mod, prev[3], prev[4])
    else:
        lead_specs, lead_args = [tile_spec], (x,)
    return pl.pallas_call(
        functools.partial(_mixer_kernel, fused),
        grid=(bsz, seq // ts),
        in_specs=lead_specs + [
            pl.BlockSpec((1, 1, 6, d), lambda b, s: (l, b, 0, 0)),
            _layer_spec(l, (d, n_in)),
            _layer_spec(l, (1, n_in)),
            _layer_spec(l, (SUBLANES * CONV_A, d)),
            _layer_spec(l, (1, d)),
            _layer_spec(l, (1, d)),
            _layer_spec(l, (1, d)),
            _layer_spec(l, (SUBLANES * CONV_B, d)),
            _layer_spec(l, (d, d)),
            _layer_spec(l, (1, d)),
            _layer_spec(l, (d, d)),
            _layer_spec(l, (d, d)),
            _layer_spec(l, (1, d)),
            _layer_spec(l, (1, d)),
        ],
        out_specs=tile_spec,
        out_shape=jax.ShapeDtypeStruct((bsz, seq, d), F32),
        scratch_shapes=[
            pltpu.VMEM((ts + CARRY_A, d), F32),
            pltpu.VMEM((ts + CARRY_B, d), F32),
            pltpu.VMEM((ts, d), F32),
            pltpu.VMEM((ts, d), F32),
            pltpu.VMEM((ts, d), F32),
            pltpu.VMEM((ts, 2 * d), F32),
        ] + ([pltpu.VMEM((ts, d), F32)] if fused else []),
        compiler_params=pltpu.CompilerParams(
            dimension_semantics=("arbitrary", "arbitrary"), vmem_limit_bytes=VMEM_LIMIT),
        name="token_mixer",
    )(*lead_args, mod, w_in, b_in, caw, cab, lag, lab, cbw, w_pa, b_pa, w_pb, w_o, g, b)


def _router_kernel(x_ref, mod_ref, rwh_ref, rwl_ref, rb_ref, xs_ref, meta_ref, tab_ref):
    tt = x_ref.shape[0]
    d = D_MODEL

    sh2 = mod_ref[0, 0, 3:4, :]
    sc2 = mod_ref[0, 0, 4:5, :]
    u2 = _ln(x_ref[...]) * (1.0 + sc2) + sh2
    u2h = u2.astype(BF16)
    u2l = (u2 - u2h.astype(F32)).astype(BF16)
    logits = (jnp.dot(u2h, rwh_ref[...], preferred_element_type=F32)
              + jnp.dot(u2l, rwh_ref[...], preferred_element_type=F32)
              + jnp.dot(u2h, rwl_ref[...], preferred_element_type=F32)) + rb_ref[...]
    lane = lax.broadcasted_iota(jnp.int32, (tt, LANES), 1)
    lane_f = lane.astype(F32)
    neg_inf = jnp.float32(-jnp.inf)
    work = jnp.where(lane < N_EXPERTS, logits, neg_inf)
    sel = jnp.zeros((tt, LANES), F32)
    vals, idxs = [], []
    for _ in range(TOP_K):
        m = jnp.max(work, axis=-1, keepdims=True)
        idx = jnp.min(jnp.where(work == m, lane_f, float(LANES)), axis=-1, keepdims=True)
        hit = lane_f == idx
        vals.append(m)
        idxs.append(idx)
        work = jnp.where(hit, neg_inf, work)
        sel = sel + hit.astype(F32)
    exps = [jnp.exp(v - vals[0]) for v in vals]
    inv = 1.0 / (exps[0] + exps[1] + exps[2] + exps[3])

    n_e = jnp.sum(sel, axis=0, keepdims=True)
    c8 = jnp.floor((n_e + float(GRANULE - 1)) * (1.0 / GRANULE))
    er = lax.broadcasted_iota(jnp.int32, (LANES, LANES), 0)
    ec = lax.broadcasted_iota(jnp.int32, (LANES, LANES), 1)
    upper = (er < ec).astype(BF16)
    off8 = jnp.dot(jnp.broadcast_to(c8, (SUBLANES, LANES)).astype(BF16), upper,
                   preferred_element_type=F32)[0:1]
    row = lax.broadcasted_iota(jnp.int32, (tt, tt), 0)
    col = lax.broadcasted_iota(jnp.int32, (tt, tt), 1)
    tri = (col < row).astype(BF16)
    prefix = jnp.dot(tri, sel.astype(BF16), preferred_element_type=F32)
    slot_dense = off8 * float(GRANULE) + prefix

    meta = jnp.zeros((tt, LANES), F32)
    for k in range(TOP_K):
        pos_k = jnp.sum(jnp.where(lane_f == idxs[k], slot_dense, 0.0), axis=-1, keepdims=True)
        meta = jnp.where(lane == k, pos_k, meta)
        meta = jnp.where(lane == TOP_K + k, exps[k] * inv, meta)
    meta_ref[...] = meta

    sub = lax.broadcasted_iota(jnp.int32, (SUBLANES, LANES), 0)
    tab_ref[0] = jnp.where(sub == 0, c8, jnp.where(sub == 1, off8, 0.0))

    meta_t = meta.T
    slot_iota = lax.broadcasted_iota(jnp.int32, (TILE_ROWS, tt), 0).astype(F32)
    pt = jnp.zeros((TILE_ROWS, tt), F32)
    gm = jnp.zeros((TILE_ROWS, tt), F32)
    for k in range(TOP_K):
        hit = slot_iota == meta_t[k:k + 1, :]
        pt = jnp.where(hit, 1.0, pt)
        gm = jnp.where(hit, meta_t[TOP_K + k:TOP_K + k + 1, :], gm)
    xs_ref[:, 0:HALF_D] = _pack_halves(jnp.dot(pt.astype(BF16), u2h, preferred_element_type=F32))
    gate_row = jnp.sum(gm, axis=-1, keepdims=True)
    xs_ref[:, HALF_D:XS_WORDS] = lax.bitcast_convert_type(
        jnp.broadcast_to(gate_row, (TILE_ROWS, LANES)), jnp.uint32)


def _router(x1, mod, l, rw_hi, rw_lo, rb_pad, seq):
    n_tok, d = x1.shape
    tt = TOK_TILE
    n_tiles = n_tok // tt
    tiles_per_seq = seq // tt
    return pl.pallas_call(
        _router_kernel,
        grid=(n_tiles,),
        in_specs=[
            pl.BlockSpec((tt, d), lambda i: (i, 0)),
            pl.BlockSpec((1, 1, 6, d), lambda i: (l, i // tiles_per_seq, 0, 0)),
            pl.BlockSpec((None, d, LANES), lambda i: (l, 0, 0)),
            pl.BlockSpec((None, d, LANES), lambda i: (l, 0, 0)),
            pl.BlockSpec((None, 1, LANES), lambda i: (l, 0, 0)),
        ],
        out_specs=[
            pl.BlockSpec((TILE_ROWS, XS_WORDS), lambda i: (i, 0)),
            pl.BlockSpec((tt, LANES), lambda i: (i, 0)),
            pl.BlockSpec((1, SUBLANES, LANES), lambda i: (i, 0, 0)),
        ],
        out_shape=[
            jax.ShapeDtypeStruct((n_tiles * TILE_ROWS, XS_WORDS), jnp.uint32),
            jax.ShapeDtypeStruct((n_tok, LANES), F32),
            jax.ShapeDtypeStruct((n_tiles, SUBLANES, LANES), F32),
        ],
        compiler_params=pltpu.CompilerParams(
            dimension_semantics=("arbitrary",), vmem_limit_bytes=VMEM_LIMIT),
        name="router",
    )(x1, mod, rw_hi, rw_lo, rb_pad)


def _granule_plan(tab, n_tiles):
    c8 = tab[:, 0, :N_EXPERTS].astype(jnp.int32).T
    off8 = tab[:, 1, :N_EXPERTS].astype(jnp.int32).T
    used = jnp.sum(c8, axis=0, keepdims=True)
    c8 = jnp.concatenate([c8, TILE_ROWS // GRANULE - used], axis=0)
    off8 = jnp.concatenate([off8, used], axis=0)
    per_e = jnp.sum(c8, axis=1)
    padded = (per_e + BLOCK_GRANULES - 1) // BLOCK_GRANULES * BLOCK_GRANULES
    e_end = jnp.cumsum(padded)
    e_start = e_end - padded
    slot = jnp.arange(N_BLOCKS * BLOCK_GRANULES, dtype=jnp.int32)
    first = lambda t: t - jnp.concatenate([jnp.zeros_like(t[..., :1]), t[..., :-1]], axis=-1)
    in_group = e_start[None, :] <= slot[:, None]
    local = slot - jnp.sum(jnp.where(in_group, first(e_start)[None, :], 0), axis=1)
    grp = jnp.sum(in_group.astype(jnp.int32), axis=1) - 1
    cum = jnp.cumsum(c8, axis=1) - c8
    base = jnp.arange(n_tiles, dtype=jnp.int32)[None, :] * TILE_ROWS + off8 * GRANULE
    tables = jnp.concatenate([cum, first(cum), first(c8), first(base)], axis=1).astype(F32)
    onehot = (grp[:, None] == jnp.arange(c8.shape[0], dtype=jnp.int32)[None, :]).astype(F32)
    picked = jnp.dot(onehot, tables, precision=lax.Precision.HIGHEST).astype(jnp.int32)
    cum_row, d_cum, d_cnt, d_base = jnp.split(picked, 4, axis=1)
    before = cum_row <= local[:, None]
    at_strip = lambda diffs: jnp.sum(jnp.where(before, diffs, 0), axis=1)
    g = local - at_strip(d_cum)
    valid = g < at_strip(d_cnt)
    rows = at_strip(d_base) + g * GRANULE
    src = jnp.where(valid, rows, 0)
    trash = n_tiles * TILE_ROWS + (slot % BLOCK_GRANULES) * GRANULE
    dst = jnp.where(valid, rows, trash)
    block_start = jnp.arange(N_BLOCKS, dtype=jnp.int32) * BLOCK_GRANULES
    block_e = jnp.minimum(
        jnp.sum((e_end[None, :] <= block_start[:, None]).astype(jnp.int32), axis=1), N_EXPERTS - 1)
    n_used = e_end[-1:] // BLOCK_GRANULES
    n_real = e_end[N_EXPERTS - 1:N_EXPERTS] // BLOCK_GRANULES
    end_blk = e_end[:N_EXPERTS] // BLOCK_GRANULES
    expert_ids = jnp.arange(N_EXPERTS, dtype=jnp.int32)
    run_end = jnp.sum(jnp.where(block_e[:, None] == expert_ids[None, :], end_blk[None, :], 0), axis=1)
    after = jnp.minimum(jnp.sum((end_blk[None, :] <= run_end[:, None]).astype(jnp.int32), axis=1), N_EXPERTS - 1)
    next_e = jnp.where(run_end < n_real, after, -1)
    i32 = lambda a: a.astype(jnp.int32)
    return i32(block_e), i32(next_e), i32(n_used), i32(n_real), i32(src), i32(dst)


def _granule_copy(hbm_ref, row, buf_ref, slot, i, sem, to_hbm):
    hbm = hbm_ref.at[pl.ds(pl.multiple_of(row, GRANULE), GRANULE)]
    vmem = buf_ref.at[slot, pl.ds(i * GRANULE, GRANULE)]
    return pltpu.make_async_copy(vmem, hbm, sem) if to_hbm else pltpu.make_async_copy(hbm, vmem, sem)


def _block_copy(hbm_ref, buf_ref, slot, sem, to_hbm):
    hbm = hbm_ref.at[pl.ds(0, EXPERT_BLOCK)]
    vmem = buf_ref.at[slot]
    return pltpu.make_async_copy(vmem, hbm, sem) if to_hbm else pltpu.make_async_copy(hbm, vmem, sem)


def _expert_kernel(layer, be_ref, ne_ref, nb_ref, nr_ref, src_ref, dst_ref, xs_ref, w1_ref, b1_ref, w2_ref,
                   b2_ref, ys_ref, w1s, w2s, w1b, w2b, xbuf, ybuf, in_sem, out_sem, w_sem):
    b = pl.program_id(0)
    n_used = nb_ref[0]
    n_real = nr_ref[0]
    e = be_ref[b]
    prev = be_ref[jnp.maximum(b - 1, 0)]
    slot = b % 2

    def start_gather(blk, s):
        for i in range(BLOCK_GRANULES):
            _granule_copy(xs_ref, src_ref[blk * BLOCK_GRANULES + i], xbuf, s, i, in_sem.at[s], False).start()

    def weight_copies(expert):
        return (pltpu.make_async_copy(w1_ref.at[layer, expert], w1s, w_sem.at[0]),
                pltpu.make_async_copy(w2_ref.at[layer, expert], w2s, w_sem.at[1]))

    @pl.when(b == 0)
    def _():
        for cp in weight_copies(e):
            cp.start()

    @pl.when(jnp.logical_and(b < n_real, jnp.logical_or(b == 0, e != prev)))
    def _():
        for cp in weight_copies(e):
            cp.wait()
        w1b[...] = w1s[...].astype(BF16)
        w2b[...] = w2s[...].astype(BF16)

        @pl.when(ne_ref[b] >= 0)
        def _():
            for cp in weight_copies(ne_ref[b]):
                cp.start()

    @pl.when(b == 0)
    def _():
        start_gather(0, 0)
        ybuf[1] = jnp.zeros(ybuf.shape[1:], jnp.uint32)
        pad_rows = ys_ref.at[pl.ds(ys_ref.shape[0] - EXPERT_BLOCK, EXPERT_BLOCK)]
        fill = pltpu.make_async_copy(ybuf.at[1], pad_rows, out_sem.at[1])
        fill.start()
        fill.wait()

    @pl.when(b + 1 < n_used)
    def _():
        start_gather(b + 1, 1 - slot)

    @pl.when(b < n_used)
    def _():
        _block_copy(xs_ref, xbuf, slot, in_sem.at[slot], False).wait()

        @pl.when(b >= 2)
        def _():
            _block_copy(ys_ref, ybuf, slot, out_sem.at[slot], True).wait()

        @pl.when(b < n_real)
        def _():
            xw = xbuf[slot]
            gate = lax.bitcast_convert_type(xw[:, HALF_D:HALF_D + 1], F32)
            h = jnp.dot(_unpack_halves(xw[:, 0:HALF_D]), w1b[...], preferred_element_type=F32) + b1_ref[0, 0]
            h_glu = jnp.minimum(h[:, 0:D_FF], SWIGLU_LIMIT)
            h_lin = jnp.clip(h[:, D_FF:2 * D_FF], -SWIGLU_LIMIT, SWIGLU_LIMIT)
            a = h_glu * _sigmoid(h_glu, SWIGLU_ALPHA) * (h_lin + 1.0)
            y = (_bdot(a, w2b[...]) + b2_ref[0, 0]) * gate
            ybuf[slot] = _pack_halves(y.astype(BF16).astype(F32))

        @pl.when(b >= n_real)
        def _():
            ybuf[slot] = jnp.zeros(ybuf.shape[1:], jnp.uint32)

        for i in range(BLOCK_GRANULES):
            _granule_copy(ys_ref, dst_ref[b * BLOCK_GRANULES + i], ybuf, slot, i, out_sem.at[slot], True).start()

    @pl.when(b == n_used - 1)
    def _():
        @pl.when(b >= 1)
        def _():
            _block_copy(ys_ref, ybuf, 1 - slot, out_sem.at[1 - slot], True).wait()

        _block_copy(ys_ref, ybuf, slot, out_sem.at[slot], True).wait()


def _experts(block_e, next_e, n_used, n_real, src, dst, xs, l, w1, b1, w2, b2, n_tiles):
    d = D_MODEL
    f2 = w1.shape[-1]
    ys_rows = n_tiles * TILE_ROWS + EXPERT_BLOCK
    expert_block = lambda i, be, ne, nb, nr, s, t: (l, be[i], 0, 0)
    grid_spec = pltpu.PrefetchScalarGridSpec(
        num_scalar_prefetch=6,
        grid=(N_BLOCKS,),
        in_specs=[
            pl.BlockSpec(memory_space=pl.ANY),
            pl.BlockSpec(memory_space=pl.ANY),
            pl.BlockSpec((1, 1, 1, f2), expert_block),
            pl.BlockSpec(memory_space=pl.ANY),
            pl.BlockSpec((1, 1, 1, d), expert_block),
        ],
        out_specs=pl.BlockSpec(memory_space=pl.ANY),
        scratch_shapes=[
            pltpu.VMEM((d, f2), F32),
            pltpu.VMEM((D_FF, d), F32),
            pltpu.VMEM((d, f2), BF16),
            pltpu.VMEM((D_FF, d), BF16),
            pltpu.VMEM((2, EXPERT_BLOCK, XS_WORDS), jnp.uint32),
            pltpu.VMEM((2, EXPERT_BLOCK, HALF_D), jnp.uint32),
            pltpu.SemaphoreType.DMA((2,)),
            pltpu.SemaphoreType.DMA((2,)),
            pltpu.SemaphoreType.DMA((2,)),
        ],
    )
    return pl.pallas_call(
        functools.partial(_expert_kernel, l),
        grid_spec=grid_spec,
        out_shape=jax.ShapeDtypeStruct((ys_rows, HALF_D), jnp.uint32),
        compiler_params=pltpu.CompilerParams(
            dimension_semantics=("arbitrary",), vmem_limit_bytes=VMEM_LIMIT),
        name="experts",
    )(block_e, next_e, n_used, n_real, src, dst, xs, w1, b1, w2, b2)


def _combine_kernel(ys_ref, meta_ref, x_ref, mod_ref, g_ref, b_ref, o_ref):
    o_ref[...] = _combine_tile(ys_ref[...], meta_ref[...], x_ref[...], mod_ref[0, 0, 5:6, :], g_ref[...], b_ref[...])


def _combine(ys, meta, x1, mod, l, g, b, seq):
    n_tok, d = x1.shape
    tt = TOK_TILE
    tiles_per_seq = seq // tt
    return pl.pallas_call(
        _combine_kernel,
        grid=(n_tok // tt,),
        in_specs=[
            pl.BlockSpec((TILE_ROWS, HALF_D), lambda i: (i, 0)),
            pl.BlockSpec((tt, LANES), lambda i: (i, 0)),
            pl.BlockSpec((tt, d), lambda i: (i, 0)),
            pl.BlockSpec((1, 1, 6, d), lambda i: (l, i // tiles_per_seq, 0, 0)),
            pl.BlockSpec((1, d), lambda i: (0, 0)),
            pl.BlockSpec((1, d), lambda i: (0, 0)),
        ],
        out_specs=pl.BlockSpec((tt, d), lambda i: (i, 0)),
        out_shape=jax.ShapeDtypeStruct((n_tok, d), F32),
        compiler_params=pltpu.CompilerParams(
            dimension_semantics=("arbitrary",), vmem_limit_bytes=VMEM_LIMIT),
        name="combine",
    )(ys, meta, x1, mod, g, b)


def kernel(x, c, ada_w, ada_b, w_in, b_in, conv_a_w, conv_a_b, ln_a_g, ln_a_b, conv_b_w, w_pa, b_pa,
           w_pb, w_o, ln1_g, ln1_b, router_w, router_b, w1, b1, w2, b2, ln2_g, ln2_b):
    bsz, seq, d = x.shape
    n_layers = ada_w.shape[0]
    n_tok = bsz * seq
    n_tiles = n_tok // TOK_TILE
    assert (n_tok, d) == (N_TOKENS, D_MODEL)

    mod = _ada_mod(c, ada_w, ada_b)

    w_in_b = w_in.astype(BF16)
    w_pa_b = w_pa.astype(BF16)
    w_pb_b = w_pb.astype(BF16)
    w_o_b = w_o.astype(BF16)
    caw = jnp.repeat(conv_a_w, SUBLANES, axis=1)
    cbw = jnp.repeat(conv_b_w, SUBLANES, axis=1)
    rw_pad = jnp.pad(router_w, ((0, 0), (0, 0), (0, LANES - N_EXPERTS)))
    rw_hi = rw_pad.astype(BF16)
    rw_lo = (rw_pad - rw_hi.astype(F32)).astype(BF16)
    rb_pad = jnp.pad(router_b, ((0, 0), (0, LANES - N_EXPERTS)))[:, None, :]
    b1r = b1.reshape(n_layers, N_EXPERTS, 1, 2 * D_FF)
    b2r = b2.reshape(n_layers, N_EXPERTS, 1, d)
    row = lambda a, l: a[l][None, :]
    rows = lambda a: a[:, None, :]

    prev = None
    for l in range(n_layers):
        x = _token_mixer(x, mod, l, w_in_b, rows(b_in), caw, rows(conv_a_b), rows(ln_a_g), rows(ln_a_b),
                         cbw, w_pa_b, rows(b_pa), w_pb_b, w_o_b, rows(ln1_g), rows(ln1_b), prev)
        x1 = x.reshape(n_tok, d)
        xs, meta, tab = _router(x1, mod, l, rw_hi, rw_lo, rb_pad, seq)
        block_e, next_e, n_used, n_real, src, dst = _granule_plan(tab, n_tiles)
        ys = _experts(block_e, next_e, n_used, n_real, src, dst, xs, l, w1, b1r, w2, b2r, n_tiles)
        prev = (ys, meta, x, rows(ln2_g), rows(ln2_b))
        x = None
    l = n_layers - 1
    x = _combine(ys, meta, x1, mod, l, row(ln2_g, l), row(ln2_b, l), seq)
    return x.reshape(bsz, seq, d)
```

```python
import functools

import jax
import jax.numpy as jnp
from jax import lax
from jax.experimental import pallas as pl
from jax.experimental.pallas import tpu as pltpu

D_MODEL = 1024
DEPTH = 4
CONV_A = 31
CONV_B = 3
N_EXPERTS = 32
TOP_K = 4
D_FF = D_MODEL
SWIGLU_LIMIT = 7.0
SWIGLU_ALPHA = 1.702
EXPERT_BLOCK = 256
DEEPNORM_ALPHA = (2.0 * DEPTH) ** 0.25
LN_EPS = 1e-5

LANES = 128
SUBLANES = 8
VMEM_LIMIT = 60 * 1024 * 1024

MIX_TS = 512
CONV_RC = 64
CONV_LC = 256
PROJ_NC = 512
CARRY_A = 32
CARRY_B = 8
TOK_TILE = 256
GRANULE = SUBLANES
BLOCK_GRANULES = EXPERT_BLOCK // GRANULE
N_TOKENS = 16384
TILE_ROWS = 1280
HALF_D = D_MODEL // 2
XS_WORDS = HALF_D + LANES
assert TILE_ROWS >= TOP_K * TOK_TILE + N_EXPERTS * (GRANULE - 1) and TILE_ROWS % LANES == 0
N_BLOCKS = -(-((N_TOKENS // TOK_TILE) * (TILE_ROWS // GRANULE)
               + (N_EXPERTS + 1) * (BLOCK_GRANULES - 1)) // BLOCK_GRANULES)

F32 = jnp.float32
BF16 = jnp.bfloat16


def _ln(x):
    mu = jnp.mean(x, axis=-1, keepdims=True)
    xc = x - mu
    var = jnp.mean(xc * xc, axis=-1, keepdims=True)
    return xc * lax.rsqrt(var + LN_EPS)


NEG_LOG2E = -1.4426950408889634


def _sigmoid(x, scale=1.0):
    return 1.0 / (1.0 + jnp.exp2(x * (scale * NEG_LOG2E)))


def _bdot(a, b):
    return jnp.dot(a.astype(BF16), b, preferred_element_type=F32)


HIGH_HALF = 0xFFFF0000


def _pack_halves(x):
    h = x.shape[1] // 2
    lo = lax.shift_right_logical(lax.bitcast_convert_type(x[:, :h], jnp.uint32), jnp.uint32(16))
    hi = lax.bitcast_convert_type(x[:, h:], jnp.uint32) & jnp.uint32(HIGH_HALF)
    return hi | lo


def _unpack_halves(w):
    lo = lax.bitcast_convert_type(lax.shift_left(w, jnp.uint32(16)), F32)
    hi = lax.bitcast_convert_type(w & jnp.uint32(HIGH_HALF), F32)
    return jnp.concatenate([lo.astype(BF16), hi.astype(BF16)], axis=1)


def _ada_kernel(c_ref, w_ref, b_ref, o_ref):
    c = c_ref[...]
    cond = c * _sigmoid(c)
    o_ref[0] = jnp.dot(cond, w_ref[0], preferred_element_type=F32,
                       precision=lax.Precision.HIGHEST) + b_ref[0]


def _ada_mod(c, ada_w, ada_b):
    bsz, d = c.shape
    n_layers = ada_w.shape[0]
    c_pad = jnp.zeros((SUBLANES, d), F32).at[:bsz].set(c)
    out = pl.pallas_call(
        _ada_kernel,
        grid=(n_layers, 6),
        in_specs=[
            pl.BlockSpec((SUBLANES, d), lambda l, j: (0, 0)),
            pl.BlockSpec((1, d, d), lambda l, j: (l, 0, j)),
            pl.BlockSpec((1, 1, d), lambda l, j: (l, 0, j)),
        ],
        out_specs=pl.BlockSpec((1, SUBLANES, d), lambda l, j: (l, 0, j)),
        out_shape=jax.ShapeDtypeStruct((n_layers, SUBLANES, 6 * d), F32),
        compiler_params=pltpu.CompilerParams(
            dimension_semantics=("arbitrary", "arbitrary"), vmem_limit_bytes=VMEM_LIMIT),
        name="ada_mod",
    )(c_pad, ada_w, ada_b.reshape(n_layers, 1, 6 * d))
    return out[:, :bsz].reshape(n_layers, bsz, 6, d)


def _combine_tile(ys_words, meta, x1, g2, g, b):
    tt = x1.shape[0]
    slot_iota = lax.broadcasted_iota(jnp.int32, (tt, TILE_ROWS), 1).astype(F32)
    p = jnp.zeros((tt, TILE_ROWS), F32)
    for k in range(TOP_K):
        p = jnp.where(slot_iota == meta[:, k:k + 1], 1.0, p)
    y = jnp.dot(p.astype(BF16), _unpack_halves(ys_words), preferred_element_type=F32)
    return _ln(DEEPNORM_ALPHA * x1 + g2 * y) * g + b


def _conv_chunk(ext_ref, w_ref, out_ref, n_taps, carry_rows, r0, c0, dep=None):
    base = carry_rows - n_taps + 1
    out = None
    for b in range(SUBLANES):
        win = CONV_RC if b == 0 else CONV_RC + SUBLANES
        z = None
        for k in range(n_taps):
            if (base + k) % SUBLANES != b:
                continue
            a8 = base + k - b
            w8 = w_ref[pl.ds(SUBLANES * k, SUBLANES), c0:c0 + CONV_LC]
            term = jnp.tile(w8, (win // SUBLANES, 1)) * ext_ref[pl.ds(r0 + a8, win), c0:c0 + CONV_LC]
            z = term if z is None else z + term
        if z is None:
            continue
        if b != 0:
            z = pltpu.roll(z, win - b, axis=0)[0:CONV_RC]
        out = z if out is None else out + z
    if dep is not None:
        out = out + jnp.tile(dep, (CONV_RC // SUBLANES, CONV_LC // LANES))
    out_ref[pl.ds(r0, CONV_RC), c0:c0 + CONV_LC] = out


def _mixer_kernel(fused, *refs):
    if fused:
        ys_ref, meta_ref, x1_ref, mod_prev_ref, g2_ref, b2_ref = refs[:6]
        refs, xin = refs[6:-1], refs[-1]
    else:
        x_ref, refs = refs[0], refs[1:]
    (mod_ref, w_in_ref, b_in_ref, caw_ref, cab_ref, lag_ref, lab_ref, cbw_ref, w_pa_ref, b_pa_ref,
     w_pb_ref, w_o_ref, g_ref, b_ref, o_ref, ext_a, ext_b, cv_a, cv_b, gb_s, sg_s) = refs
    ts = o_ref.shape[1]
    d = D_MODEL

    @pl.when(pl.program_id(1) == 0)
    def _():
        ext_a[pl.ds(0, CARRY_A), :] = jnp.zeros((CARRY_A, d), F32)
        ext_b[pl.ds(0, CARRY_B), :] = jnp.zeros((CARRY_B, d), F32)

    if fused:
        for t in range(ts // TOK_TILE):
            rows = pl.ds(t * TOK_TILE, TOK_TILE)
            xin[rows, :] = _combine_tile(
                ys_ref[pl.ds(t * TILE_ROWS, TILE_ROWS), :], meta_ref[rows, :], x1_ref[0, rows, :],
                mod_prev_ref[0, 0, 5:6, :], g2_ref[...], b2_ref[...])
        xt = xin[...]
    else:
        xt = x_ref[0]
    sh1 = mod_ref[0, 0, 0:1, :]
    sc1 = mod_ref[0, 0, 1:2, :]
    g1 = mod_ref[0, 0, 2:3, :]
    u = (_ln(xt) * (1.0 + sc1) + sh1).astype(BF16)

    def proj(c0, n):
        return jnp.dot(u, w_in_ref[:, c0:c0 + n], preferred_element_type=F32) + b_in_ref[:, c0:c0 + n]

    cur_a = pl.ds(CARRY_A, ts)
    cur_b = pl.ds(CARRY_B, ts)
    nc = PROJ_NC

    def tie(z):
        bits = lax.bitcast_convert_type(z[0:SUBLANES, 0:LANES], jnp.uint32)
        zero = lax.shift_right_logical(lax.shift_right_logical(bits, jnp.uint32(16)), jnp.uint32(16))
        return lax.bitcast_convert_type(zero, F32)

    za = proj(0, 2 * d)
    ext_a[cur_a, :] = za[:, 0:d] * _sigmoid(za[:, d:2 * d])

    def b_gate(c):
        z = proj(2 * d + c, nc)
        gb_s[:, c:c + nc] = z
        return tie(z)

    def b_c(c):
        z = proj(3 * d + c, nc)
        ext_b[cur_b, c:c + nc] = z
        return tie(z)

    def b_h(c):
        z = proj(4 * d + c, nc)
        ext_b[cur_b, c:c + nc] = ext_b[cur_b, c:c + nc] * z
        return tie(z)

    def merge_gate(c):
        z = proj(5 * d + c, nc)
        sg_s[:, c:c + nc] = _sigmoid(z)
        return tie(z)

    jobs = [lambda c=c, f=f: f(c) for f in (b_gate, b_c, b_h) for c in range(0, d, nc)]
    jobs += [lambda c=c: merge_gate(c) for c in range(0, 2 * d, nc)]
    conv_chunks = [(r0, c0) for r0 in range(0, ts, CONV_RC) for c0 in range(0, d, CONV_LC)]
    assert len(jobs) < len(conv_chunks)
    stride = (len(conv_chunks) - 1) // len(jobs)
    dep = None
    for i, (r0, c0) in enumerate(conv_chunks):
        nxt = jobs[i // stride]() if i % stride == 0 and i // stride < len(jobs) else None
        _conv_chunk(ext_a, caw_ref, cv_a, CONV_A, CARRY_A, r0, c0, dep)
        dep = nxt
    for r0, c0 in conv_chunks:
        _conv_chunk(ext_b, cbw_ref, cv_b, CONV_B, CARRY_B, r0, c0)

    ext_a[pl.ds(0, CARRY_A), :] = ext_a[pl.ds(ts, CARRY_A), :]
    ext_b[pl.ds(0, CARRY_B), :] = ext_b[pl.ds(ts, CARRY_B), :]

    ya = _ln(cv_a[...] + cab_ref[...]) * lag_ref[...] + lab_ref[...]
    ya = ya * _sigmoid(ya)
    ya = _bdot(ya, w_pa_ref[...]) + b_pa_ref[...]
    yb = _bdot(gb_s[...] * cv_b[...], w_pb_ref[...])

    m = sg_s[:, 0:d] * ya + sg_s[:, d:2 * d] * yb
    y = _bdot(m, w_o_ref[...])
    o_ref[0] = _ln(DEEPNORM_ALPHA * xt + g1 * y) * g_ref[...] + b_ref[...]


def _layer_spec(l, shape):
    nd = len(shape)
    return pl.BlockSpec((None,) + shape, lambda b, s: (l,) + (0,) * nd, pipeline_mode=pl.Buffered(1))


def _token_mixer(x, mod, l, w_in, b_in, caw, cab, lag, lab, cbw, w_pa, b_pa, w_pb, w_o, g, b, prev=None):
    fused = prev is not None
    bsz, seq, d = prev[2].shape if fused else x.shape
    ts = MIX_TS
    n_in = w_in.shape[-1]
    tile_spec = pl.BlockSpec((1, ts, d), lambda b, s: (b, s, 0))
    if fused:
        steps = seq // ts
        lead_specs = [
            pl.BlockSpec((ts // TOK_TILE * TILE_ROWS, HALF_D), lambda b, s: (b * steps + s, 0)),
            pl.BlockSpec((ts, LANES), lambda b, s: (b * steps + s, 0)),
            tile_spec,
            pl.BlockSpec((1, 1, 6, d), lambda b, s: (l - 1, b, 0, 0)),
            _layer_spec(l - 1, (1, d)),
            _layer_spec(l - 1, (1, d)),
        ]
        lead_args = (prev[0], prev[1], prev[2], mod, prev[3], prev[4])
    else:
        lead_specs, lead_args = [tile_spec], (x,)
    return pl.pallas_call(
        functools.partial(_mixer_kernel, fused),
        grid=(bsz, seq // ts),
        in_specs=lead_specs + [
            pl.BlockSpec((1, 1, 6, d), lambda b, s: (l, b, 0, 0)),
            _layer_spec(l, (d, n_in)),
            _layer_spec(l, (1, n_in)),
            _layer_spec(l, (SUBLANES * CONV_A, d)),
            _layer_spec(l, (1, d)),
            _layer_spec(l, (1, d)),
            _layer_spec(l, (1, d)),
            _layer_spec(l, (SUBLANES * CONV_B, d)),
            _layer_spec(l, (d, d)),
            _layer_spec(l, (1, d)),
            _layer_spec(l, (d, d)),
            _layer_spec(l, (d, d)),
            _layer_spec(l, (1, d)),
            _layer_spec(l, (1, d)),
        ],
        out_specs=tile_spec,
        out_shape=jax.ShapeDtypeStruct((bsz, seq, d), F32),
        scratch_shapes=[
            pltpu.VMEM((ts + CARRY_A, d), F32),
            pltpu.VMEM((ts + CARRY_B, d), F32),
            pltpu.VMEM((ts, d), F32),
            pltpu.VMEM((ts, d), F32),
            pltpu.VMEM((ts, d), F32),
            pltpu.VMEM((ts, 2 * d), F32),
        ] + ([pltpu.VMEM((ts, d), F32)] if fused else []),
        compiler_params=pltpu.CompilerParams(
            dimension_semantics=("arbitrary", "arbitrary"), vmem_limit_bytes=VMEM_LIMIT),
        name="token_mixer",
    )(*lead_args, mod, w_in, b_in, caw, cab, lag, lab, cbw, w_pa, b_pa, w_pb, w_o, g, b)


def _router_kernel(x_ref, mod_ref, rwh_ref, rwl_ref, rb_ref, xs_ref, meta_ref, tab_ref):
    tt = x_ref.shape[0]
    d = D_MODEL

    sh2 = mod_ref[0, 0, 3:4, :]
    sc2 = mod_ref[0, 0, 4:5, :]
    u2 = _ln(x_ref[...]) * (1.0 + sc2) + sh2
    u2h = u2.astype(BF16)
    u2l = (u2 - u2h.astype(F32)).astype(BF16)
    logits = (jnp.dot(u2h, rwh_ref[...], preferred_element_type=F32)
              + jnp.dot(u2l, rwh_ref[...], preferred_element_type=F32)
              + jnp.dot(u2h, rwl_ref[...], preferred_element_type=F32)) + rb_ref[...]
    lane = lax.broadcasted_iota(jnp.int32, (tt, LANES), 1)
    lane_f = lane.astype(F32)
    neg_inf = jnp.float32(-jnp.inf)
    work = jnp.where(lane < N_EXPERTS, logits, neg_inf)
    sel = jnp.zeros((tt, LANES), F32)
    vals, idxs = [], []
    for _ in range(TOP_K):
        m = jnp.max(work, axis=-1, keepdims=True)
        idx = jnp.min(jnp.where(work == m, lane_f, float(LANES)), axis=-1, keepdims=True)
        hit = lane_f == idx
        vals.append(m)
        idxs.append(idx)
        work = jnp.where(hit, neg_inf, work)
        sel = sel + hit.astype(F32)
    exps = [jnp.exp(v - vals[0]) for v in vals]
    inv = 1.0 / (exps[0] + exps[1] + exps[2] + exps[3])

    n_e = jnp.sum(sel, axis=0, keepdims=True)
    c8 = jnp.floor((n_e + float(GRANULE - 1)) * (1.0 / GRANULE))
    er = lax.broadcasted_iota(jnp.int32, (LANES, LANES), 0)
    ec = lax.broadcasted_iota(jnp.int32, (LANES, LANES), 1)
    upper = (er < ec).astype(BF16)
    off8 = jnp.dot(jnp.broadcast_to(c8, (SUBLANES, LANES)).astype(BF16), upper,
                   preferred_element_type=F32)[0:1]
    row = lax.broadcasted_iota(jnp.int32, (tt, tt), 0)
    col = lax.broadcasted_iota(jnp.int32, (tt, tt), 1)
    tri = (col < row).astype(BF16)
    prefix = jnp.dot(tri, sel.astype(BF16), preferred_element_type=F32)
    slot_dense = off8 * float(GRANULE) + prefix

    meta = jnp.zeros((tt, LANES), F32)
    for k in range(TOP_K):
        pos_k = jnp.sum(jnp.where(lane_f == idxs[k], slot_dense, 0.0), axis=-1, keepdims=True)
        meta = jnp.where(lane == k, pos_k, meta)
        meta = jnp.where(lane == TOP_K + k, exps[k] * inv, meta)
    meta_ref[...] = meta

    sub = lax.broadcasted_iota(jnp.int32, (SUBLANES, LANES), 0)
    tab_ref[0] = jnp.where(sub == 0, c8, jnp.where(sub == 1, off8, 0.0))

    meta_t = meta.T
    slot_iota = lax.broadcasted_iota(jnp.int32, (TILE_ROWS, tt), 0).astype(F32)
    pt = jnp.zeros((TILE_ROWS, tt), F32)
    gm = jnp.zeros((TILE_ROWS, tt), F32)
    for k in range(TOP_K):
        hit = slot_iota == meta_t[k:k + 1, :]
        pt = jnp.where(hit, 1.0, pt)
        gm = jnp.where(hit, meta_t[TOP_K + k:TOP_K + k + 1, :], gm)
    xs_ref[:, 0:HALF_D] = _pack_halves(jnp.dot(pt.astype(BF16), u2h, preferred_element_type=F32))
    gate_row = jnp.sum(gm, axis=-1, keepdims=True)
    xs_ref[:, HALF_D:XS_WORDS] = lax.bitcast_convert_type(
        jnp.broadcast_to(gate_row, (TILE_ROWS, LANES)), jnp.uint32)


def _router(x1, mod, l, rw_hi, rw_lo, rb_pad, seq):
    n_tok, d = x1.shape
    tt = TOK_TILE
    n_tiles = n_tok // tt
    tiles_per_seq = seq // tt
    return pl.pallas_call(
        _router_kernel,
        grid=(n_tiles,),
        in_specs=[
            pl.BlockSpec((tt, d), lambda i: (i, 0)),
            pl.BlockSpec((1, 1, 6, d), lambda i: (l, i // tiles_per_seq, 0, 0)),
            pl.BlockSpec((None, d, LANES), lambda i: (l, 0, 0)),
            pl.BlockSpec((None, d, LANES), lambda i: (l, 0, 0)),
            pl.BlockSpec((None, 1, LANES), lambda i: (l, 0, 0)),
        ],
        out_specs=[
            pl.BlockSpec((TILE_ROWS, XS_WORDS), lambda i: (i, 0)),
            pl.BlockSpec((tt, LANES), lambda i: (i, 0)),
            pl.BlockSpec((1, SUBLANES, LANES), lambda i: (i, 0, 0)),
        ],
        out_shape=[
            jax.ShapeDtypeStruct((n_tiles * TILE_ROWS, XS_WORDS), jnp.uint32),
            jax.ShapeDtypeStruct((n_tok, LANES), F32),
            jax.ShapeDtypeStruct((n_tiles, SUBLANES, LANES), F32),
        ],
        compiler_params=pltpu.CompilerParams(
            dimension_semantics=("arbitrary",), vmem_limit_bytes=VMEM_LIMIT),
        name="router",
    )(x1, mod, rw_hi, rw_lo, rb_pad)


def _granule_plan(tab, n_tiles):
    c8 = tab[:, 0, :N_EXPERTS].astype(jnp.int32).T
    off8 = tab[:, 1, :N_EXPERTS].astype(jnp.int32).T
    used = jnp.sum(c8, axis=0, keepdims=True)
    c8 = jnp.concatenate([c8, TILE_ROWS // GRANULE - used], axis=0)
    off8 = jnp.concatenate([off8, used], axis=0)
    per_e = jnp.sum(c8, axis=1)
    padded = (per_e + BLOCK_GRANULES - 1) // BLOCK_GRANULES * BLOCK_GRANULES
    e_end = jnp.cumsum(padded)
    e_start = e_end - padded
    slot = jnp.arange(N_BLOCKS * BLOCK_GRANULES, dtype=jnp.int32)
    first = lambda t: t - jnp.concatenate([jnp.zeros_like(t[..., :1]), t[..., :-1]], axis=-1)
    in_group = e_start[None, :] <= slot[:, None]
    local = slot - jnp.sum(jnp.where(in_group, first(e_start)[None, :], 0), axis=1)
    grp = jnp.sum(in_group.astype(jnp.int32), axis=1) - 1
    cum = jnp.cumsum(c8, axis=1) - c8
    base = jnp.arange(n_tiles, dtype=jnp.int32)[None, :] * TILE_ROWS + off8 * GRANULE
    tables = jnp.concatenate([cum, first(cum), first(c8), first(base)], axis=1).astype(F32)
    onehot = (grp[:, None] == jnp.arange(c8.shape[0], dtype=jnp.int32)[None, :]).astype(F32)
    picked = jnp.dot(onehot, tables, precision=lax.Precision.HIGHEST).astype(jnp.int32)
    cum_row, d_cum, d_cnt, d_base = jnp.split(picked, 4, axis=1)
    before = cum_row <= local[:, None]
    at_strip = lambda diffs: jnp.sum(jnp.where(before, diffs, 0), axis=1)
    g = local - at_strip(d_cum)
    valid = g < at_strip(d_cnt)
    rows = at_strip(d_base) + g * GRANULE
    src = jnp.where(valid, rows, 0)
    trash = n_tiles * TILE_ROWS + (slot % BLOCK_GRANULES) * GRANULE
    dst = jnp.where(valid, rows, trash)
    block_start = jnp.arange(N_BLOCKS, dtype=jnp.int32) * BLOCK_GRANULES
    block_e = jnp.minimum(
        jnp.sum((e_end[None, :] <= block_start[:, None]).astype(jnp.int32), axis=1), N_EXPERTS - 1)
    n_used = e_end[-1:] // BLOCK_GRANULES
    n_real = e_end[N_EXPERTS - 1:N_EXPERTS] // BLOCK_GRANULES
    end_blk = e_end[:N_EXPERTS] // BLOCK_GRANULES
    expert_ids = jnp.arange(N_EXPERTS, dtype=jnp.int32)
    run_end = jnp.sum(jnp.where(block_e[:, None] == expert_ids[None, :], end_blk[None, :], 0), axis=1)
    after = jnp.minimum(jnp.sum((end_blk[None, :] <= run_end[:, None]).astype(jnp.int32), axis=1), N_EXPERTS - 1)
    next_e = jnp.where(run_end < n_real, after, -1)
    i32 = lambda a: a.astype(jnp.int32)
    return i32(block_e), i32(next_e), i32(n_used), i32(n_real), i32(src), i32(dst)


def _granule_copy(hbm_ref, row, buf_ref, slot, i, sem, to_hbm):
    hbm = hbm_ref.at[pl.ds(pl.multiple_of(row, GRANULE), GRANULE)]
    vmem = buf_ref.at[slot, pl.ds(i * GRANULE, GRANULE)]
    return pltpu.make_async_copy(vmem, hbm, sem) if to_hbm else pltpu.make_async_copy(hbm, vmem, sem)


def _block_copy(hbm_ref, buf_ref, slot, sem, to_hbm):
    hbm = hbm_ref.at[pl.ds(0, EXPERT_BLOCK)]
    vmem = buf_ref.at[slot]
    return pltpu.make_async_copy(vmem, hbm, sem) if to_hbm else pltpu.make_async_copy(hbm, vmem, sem)


def _expert_kernel(layer, be_ref, ne_ref, nb_ref, nr_ref, src_ref, dst_ref, xs_ref, w1_ref, b1_ref, w2_ref,
                   b2_ref, ys_ref, w1s, w2s, w1b, w2b, xbuf, ybuf, in_sem, out_sem, w_sem):
    b = pl.program_id(0)
    n_used = nb_ref[0]
    n_real = nr_ref[0]
    e = be_ref[b]
    prev = be_ref[jnp.maximum(b - 1, 0)]
    slot = b % 2

    def start_gather(blk, s):
        for i in range(BLOCK_GRANULES):
            _granule_copy(xs_ref, src_ref[blk * BLOCK_GRANULES + i], xbuf, s, i, in_sem.at[s], False).start(
                priority=i % 2)

    def weight_copies(expert):
        return (pltpu.make_async_copy(w1_ref.at[layer, expert], w1s, w_sem.at[0]),
                pltpu.make_async_copy(w2_ref.at[layer, expert], w2s, w_sem.at[1]))

    @pl.when(b == 0)
    def _():
        for cp in weight_copies(e):
            cp.start()

    @pl.when(jnp.logical_and(b < n_real, jnp.logical_or(b == 0, e != prev)))
    def _():
        for cp in weight_copies(e):
            cp.wait()
        w1b[...] = w1s[...].astype(BF16)
        w2b[...] = w2s[...].astype(BF16)

        @pl.when(ne_ref[b] >= 0)
        def _():
            for cp in weight_copies(ne_ref[b]):
                cp.start()

    @pl.when(b == 0)
    def _():
        start_gather(0, 0)
        ybuf[1] = jnp.zeros(ybuf.shape[1:], jnp.uint32)
        pad_rows = ys_ref.at[pl.ds(ys_ref.shape[0] - EXPERT_BLOCK, EXPERT_BLOCK)]
        fill = pltpu.make_async_copy(ybuf.at[1], pad_rows, out_sem.at[1])
        fill.start()
        fill.wait()

    @pl.when(b + 1 < n_used)
    def _():
        start_gather(b + 1, 1 - slot)

    @pl.when(b < n_used)
    def _():
        _block_copy(xs_ref, xbuf, slot, in_sem.at[slot], False).wait()

        @pl.when(b >= 2)
        def _():
            _block_copy(ys_ref, ybuf, slot, out_sem.at[slot], True).wait()

        @pl.when(b < n_real)
        def _():
            xw = xbuf[slot]
            gate = lax.bitcast_convert_type(xw[:, HALF_D:HALF_D + 1], F32)
            h = jnp.dot(_unpack_halves(xw[:, 0:HALF_D]), w1b[...], preferred_element_type=F32) + b1_ref[0, 0]
            h_glu = jnp.minimum(h[:, 0:D_FF], SWIGLU_LIMIT)
            h_lin = jnp.clip(h[:, D_FF:2 * D_FF], -SWIGLU_LIMIT, SWIGLU_LIMIT)
            a = h_glu * _sigmoid(h_glu, SWIGLU_ALPHA) * (h_lin + 1.0)
            y = (_bdot(a, w2b[...]) + b2_ref[0, 0]) * gate
            ybuf[slot] = _pack_halves(y.astype(BF16).astype(F32))

        @pl.when(b >= n_real)
        def _():
            ybuf[slot] = jnp.zeros(ybuf.shape[1:], jnp.uint32)

        for i in range(BLOCK_GRANULES):
            _granule_copy(ys_ref, dst_ref[b * BLOCK_GRANULES + i], ybuf, slot, i, out_sem.at[slot], True).start(
                priority=i % 2)

    @pl.when(b == n_used - 1)
    def _():
        @pl.when(b >= 1)
        def _():
            _block_copy(ys_ref, ybuf, 1 - slot, out_sem.at[1 - slot], True).wait()

        _block_copy(ys_ref, ybuf, slot, out_sem.at[slot], True).wait()


def _experts(block_e, next_e, n_used, n_real, src, dst, xs, l, w1, b1, w2, b2, n_tiles):
    d = D_MODEL
    f2 = w1.shape[-1]
    ys_rows = n_tiles * TILE_ROWS + EXPERT_BLOCK
    expert_block = lambda i, be, ne, nb, nr, s, t: (l, be[i], 0, 0)
    grid_spec = pltpu.PrefetchScalarGridSpec(
        num_scalar_prefetch=6,
        grid=(N_BLOCKS,),
        in_specs=[
            pl.BlockSpec(memory_space=pl.ANY),
            pl.BlockSpec(memory_space=pl.ANY),
            pl.BlockSpec((1, 1, 1, f2), expert_block),
            pl.BlockSpec(memory_space=pl.ANY),
            pl.BlockSpec((1, 1, 1, d), expert_block),
        ],
        out_specs=pl.BlockSpec(memory_space=pl.ANY),
        scratch_shapes=[
            pltpu.VMEM((d, f2), F32),
            pltpu.VMEM((D_FF, d), F32),
            pltpu.VMEM((d, f2), BF16),
            pltpu.VMEM((D_FF, d), BF16),
            pltpu.VMEM((2, EXPERT_BLOCK, XS_WORDS), jnp.uint32),
            pltpu.VMEM((2, EXPERT_BLOCK, HALF_D), jnp.uint32),
            pltpu.SemaphoreType.DMA((2,)),
            pltpu.SemaphoreType.DMA((2,)),
            pltpu.SemaphoreType.DMA((2,)),
        ],
    )
    return pl.pallas_call(
        functools.partial(_expert_kernel, l),
        grid_spec=grid_spec,
        out_shape=jax.ShapeDtypeStruct((ys_rows, HALF_D), jnp.uint32),
        compiler_params=pltpu.CompilerParams(
            dimension_semantics=("arbitrary",), vmem_limit_bytes=VMEM_LIMIT),
        name="experts",
    )(block_e, next_e, n_used, n_real, src, dst, xs, w1, b1, w2, b2)


def _combine_kernel(ys_ref, meta_ref, x_ref, mod_ref, g_ref, b_ref, o_ref):
    o_ref[...] = _combine_tile(ys_ref[...], meta_ref[...], x_ref[...], mod_ref[0, 0, 5:6, :], g_ref[...], b_ref[...])


def _combine(ys, meta, x1, mod, l, g, b, seq):
    n_tok, d = x1.shape
    tt = TOK_TILE
    tiles_per_seq = seq // tt
    return pl.pallas_call(
        _combine_kernel,
        grid=(n_tok // tt,),
        in_specs=[
            pl.BlockSpec((TILE_ROWS, HALF_D), lambda i: (i, 0)),
            pl.BlockSpec((tt, LANES), lambda i: (i, 0)),
            pl.BlockSpec((tt, d), lambda i: (i, 0)),
            pl.BlockSpec((1, 1, 6, d), lambda i: (l, i // tiles_per_seq, 0, 0)),
            pl.BlockSpec((1, d), lambda i: (0, 0)),
            pl.BlockSpec((1, d), lambda i: (0, 0)),
        ],
        out_specs=pl.BlockSpec((tt, d), lambda i: (i, 0)),
        out_shape=jax.ShapeDtypeStruct((n_tok, d), F32),
        compiler_params=pltpu.CompilerParams(
            dimension_semantics=("arbitrary",), vmem_limit_bytes=VMEM_LIMIT),
        name="combine",
    )(ys, meta, x1, mod, g, b)


def kernel(x, c, ada_w, ada_b, w_in, b_in, conv_a_w, conv_a_b, ln_a_g, ln_a_b, conv_b_w, w_pa, b_pa,
           w_pb, w_o, ln1_g, ln1_b, router_w, router_b, w1, b1, w2, b2, ln2_g, ln2_b):
    bsz, seq, d = x.shape
    n_layers = ada_w.shape[0]
    n_tok = bsz * seq
    n_tiles = n_tok // TOK_TILE
    assert (n_tok, d) == (N_TOKENS, D_MODEL)

    mod = _ada_mod(c, ada_w, ada_b)

    w_in_b = w_in.astype(BF16)
    w_pa_b = w_pa.astype(BF16)
    w_pb_b = w_pb.astype(BF16)
    w_o_b = w_o.astype(BF16)
    caw = jnp.repeat(conv_a_w, SUBLANES, axis=1)
    cbw = jnp.repeat(conv_b_w, SUBLANES, axis=1)
    rw_pad = jnp.pad(router_w, ((0, 0), (0, 0), (0, LANES - N_EXPERTS)))
    rw_hi = rw_pad.astype(BF16)
    rw_lo = (rw_pad - rw_hi.astype(F32)).astype(BF16)
    rb_pad = jnp.pad(router_b, ((0, 0), (0, LANES - N_EXPERTS)))[:, None, :]
    b1r = b1.reshape(n_layers, N_EXPERTS, 1, 2 * D_FF)
    b2r = b2.reshape(n_layers, N_EXPERTS, 1, d)
    row = lambda a, l: a[l][None, :]
    rows = lambda a: a[:, None, :]

    prev = None
    for l in range(n_layers):
        x = _token_mixer(x, mod, l, w_in_b, rows(b_in), caw, rows(conv_a_b), rows(ln_a_g), rows(ln_a_b),
                         cbw, w_pa_b, rows(b_pa), w_pb_b, w_o_b, rows(ln1_g), rows(ln1_b), prev)
        x1 = x.reshape(n_tok, d)
        xs, meta, tab = _router(x1, mod, l, rw_hi, rw_lo, rb_pad, seq)
        block_e, next_e, n_used, n_real, src, dst = _granule_plan(tab, n_tiles)
        ys = _experts(block_e, next_e, n_used, n_real, src, dst, xs, l, w1, b1r, w2, b2r, n_tiles)
        prev = (ys, meta, x, rows(ln2_g), rows(ln2_b))
        x = None
    l = n_layers - 1
    x = _combine(ys, meta, x1, mod, l, row(ln2_g, l), row(ln2_b, l), seq)
    return x.reshape(bsz, seq, d)
```
